```python
import jax, jax.numpy as jnp
from jax import lax
import numpy as np

D_MODEL = 2048
BATCH = 4
SEQ = 4096
DEPTH = 1

HEAD_DIM = 128
GDN_HEADS = 8
GDN_WIDTH = GDN_HEADS * HEAD_DIM
GDN_CONV = 4
GDN_CHUNK = 64
MOBA_HEADS = 8
MOBA_WIDTH = MOBA_HEADS * HEAD_DIM
MOBA_BLOCK = 256
MOBA_TOPK = 3
MOBA_QCHUNK = 32
ROPE_THETA = 10000.0
EPS = 1e-6
N_BRANCH = 2

SPLIT_SIZES = [3 * GDN_WIDTH, GDN_WIDTH, GDN_HEADS, GDN_HEADS,
               3 * MOBA_WIDTH, MOBA_WIDTH, N_BRANCH * D_MODEL]
SPLIT_POINTS = [int(s) for s in np.cumsum(SPLIT_SIZES)[:-1]]
IN_COLS = int(sum(SPLIT_SIZES))

kernel_name = "hybrid_gdn_moba_gated_merge"


def rms_norm(x, w):
    xf = x.astype(jnp.float32)
    y = xf * lax.rsqrt(jnp.mean(xf * xf, axis=-1, keepdims=True) + EPS) * w.astype(jnp.float32)
    return y.astype(x.dtype)


def l2_norm(x):
    xf = x.astype(jnp.float32)
    return xf * lax.rsqrt(jnp.sum(xf * xf, axis=-1, keepdims=True) + EPS)


def causal_depthwise_conv(x, w):
    k_width, ch = w.shape
    return lax.conv_general_dilated(
        x, w[:, None, :].astype(x.dtype), window_strides=(1,), padding=[(k_width - 1, 0)],
        dimension_numbers=("NWC", "WIO", "NWC"), feature_group_count=ch)


def rope(x):
    t_len, d = x.shape[1], x.shape[3]
    inv_freq = ROPE_THETA ** (-jnp.arange(0, d, 2, dtype=jnp.float32) / d)
    ang = jnp.arange(t_len, dtype=jnp.float32)[:, None] * inv_freq[None, :]
    cos = jnp.cos(ang)[None, :, None, :]
    sin = jnp.sin(ang)[None, :, None, :]
    xf = x.astype(jnp.float32)
    x1, x2 = xf[..., : d // 2], xf[..., d // 2:]
    return jnp.concatenate([x1 * cos - x2 * sin, x2 * cos + x1 * sin], axis=-1).astype(x.dtype)


def gated_delta_rule(q, k, v, g, beta):
    b_sz, t_len, n_h, d_k = q.shape
    d_v = v.shape[-1]
    c = GDN_CHUNK
    n_c = t_len // c

    def to_chunks(t):
        return t.astype(jnp.float32).reshape(b_sz, n_c, c, n_h, -1).transpose(1, 0, 3, 2, 4)

    q = to_chunks(q) * (d_k ** -0.5)
    k = to_chunks(k)
    v = to_chunks(v)
    g = g.astype(jnp.float32).reshape(b_sz, n_c, c, n_h).transpose(1, 0, 3, 2)
    beta = beta.astype(jnp.float32).reshape(b_sz, n_c, c, n_h).transpose(1, 0, 3, 2)

    cum_g = jnp.cumsum(g, axis=-1)
    causal = jnp.tril(jnp.ones((c, c), dtype=bool))
    strict = jnp.tril(jnp.ones((c, c), dtype=bool), k=-1)
    decay_mat = jnp.exp(jnp.where(causal, cum_g[..., :, None] - cum_g[..., None, :], -jnp.inf))

    kk = jnp.einsum("nbhid,nbhjd->nbhij", k, k)
    a_mat = jnp.where(strict, beta[..., :, None] * kk * decay_mat, 0.0)
    eye = jnp.eye(c, dtype=jnp.float32)
    rhs = jnp.concatenate([v * beta[..., None], k * (beta * jnp.exp(cum_g))[..., None]], axis=-1)
    sol = lax.linalg.triangular_solve(eye + a_mat, rhs, left_side=True, lower=True,
                                      unit_diagonal=True)
    u, w = sol[..., :d_v], sol[..., d_v:]

    qk = jnp.einsum("nbhid,nbhjd->nbhij", q, k) * decay_mat
    q_dec = q * jnp.exp(cum_g)[..., None]
    k_dec = k * jnp.exp(cum_g[..., -1:] - cum_g)[..., None]
    g_last = jnp.exp(cum_g[..., -1])

    def step(state, xs):
        q_c, k_c, u_c, w_c, qk_c, gl_c = xs
        v_new = u_c - jnp.einsum("bhcd,bhde->bhce", w_c, state)
        o_c = jnp.einsum("bhcd,bhde->bhce", q_c, state) + jnp.einsum("bhij,bhje->bhie", qk_c, v_new)
        state = state * gl_c[..., None, None] + jnp.einsum("bhcd,bhce->bhde", k_c, v_new)
        return state, o_c

    s0 = jnp.zeros((b_sz, n_h, d_k, d_v), jnp.float32)
    _, o = lax.scan(step, s0, (q_dec, k_dec, u, w, qk, g_last))
    return o.transpose(1, 0, 3, 2, 4).reshape(b_sz, t_len, n_h, d_v)


def moba_attention(q, k, v):
    b_sz, t_len, n_h, d = q.shape
    bs = MOBA_BLOCK
    n_blk = -(-t_len // bs)
    pad = n_blk * bs - t_len
    n_sel = max(1, min(MOBA_TOPK, n_blk - 1))
    qc = MOBA_QCHUNK
    n_qc = t_len // qc
    scale = d ** -0.5

    q = q.transpose(0, 2, 1, 3)
    k = jnp.pad(k.transpose(0, 2, 1, 3), ((0, 0), (0, 0), (0, pad), (0, 0)))
    v = jnp.pad(v.transpose(0, 2, 1, 3), ((0, 0), (0, 0), (0, pad), (0, 0)))
    k_blocks = k.reshape(b_sz, n_h, n_blk, bs, d)
    v_blocks = v.reshape(b_sz, n_h, n_blk, bs, d)
    k_mean = jnp.mean(k_blocks.astype(jnp.float32), axis=3)

    q_blk = jnp.arange(t_len) // bs
    gate = jnp.einsum("bhtd,bhnd->bhtn", q.astype(jnp.float32), k_mean)
    gate = jnp.where(jnp.arange(n_blk)[None, :] < q_blk[:, None], gate, -jnp.inf)
    _, sel = lax.top_k(gate, n_sel)

    bi = jnp.arange(b_sz)[:, None, None, None]
    hi = jnp.arange(n_h)[None, :, None, None]

    def attend(args):
        q_c, sel_c, start = args
        pos = start + jnp.arange(qc)
        blk = start // bs
        k_own = lax.dynamic_slice_in_dim(k, blk * bs, bs, axis=2)
        v_own = lax.dynamic_slice_in_dim(v, blk * bs, bs, axis=2)
        s_own = jnp.einsum("bhqd,bhkd->bhqk", q_c, k_own).astype(jnp.float32) * scale
        s_own = jnp.where((blk * bs + jnp.arange(bs))[None, :] <= pos[:, None], s_own, -jnp.inf)
        k_sel = k_blocks[bi, hi, sel_c]
        v_sel = v_blocks[bi, hi, sel_c]
        s_sel = jnp.einsum("bhqd,bhqnkd->bhqnk", q_c, k_sel).astype(jnp.float32) * scale
        valid = jnp.arange(n_sel) < blk
        s_sel = jnp.where(valid[None, None, None, :, None], s_sel, -jnp.inf)
        s_sel = s_sel.reshape(b_sz, n_h, qc, n_sel * bs)
        p = jax.nn.softmax(jnp.concatenate([s_sel, s_own], axis=-1), axis=-1)
        p_sel = p[..., : n_sel * bs].reshape(b_sz, n_h, qc, n_sel, bs).astype(v.dtype)
        p_own = p[..., n_sel * bs:].astype(v.dtype)
        return (jnp.einsum("bhqnk,bhqnkd->bhqd", p_sel, v_sel)
                + jnp.einsum("bhqk,bhkd->bhqd", p_own, v_own))

    q_chunks = q.reshape(b_sz, n_h, n_qc, qc, d).transpose(2, 0, 1, 3, 4)
    sel_chunks = sel.reshape(b_sz, n_h, n_qc, qc, n_sel).transpose(2, 0, 1, 3, 4)
    starts = jnp.arange(n_qc, dtype=jnp.int32) * qc
    o = lax.map(attend, (q_chunks, sel_chunks, starts))
    return o.transpose(1, 0, 3, 2, 4).reshape(b_sz, t_len, n_h, d)


def setup_inputs(seed: int = 0) -> dict:
    key = jax.random.key(seed)
    ks = jax.random.split(key, 14)
    f32 = jnp.float32
    x = jax.random.normal(ks[0], (BATCH, SEQ, D_MODEL), f32)
    norm_w = 1.0 + 0.02 * jax.random.normal(ks[1], (DEPTH, D_MODEL), f32)
    w_in = jax.random.normal(ks[2], (DEPTH, D_MODEL, IN_COLS), f32) * D_MODEL ** -0.5
    gdn_conv_w = jax.random.normal(ks[3], (DEPTH, GDN_CONV, 3 * GDN_WIDTH), f32) * GDN_CONV ** -0.5
    gdn_a_log = jnp.log(jax.random.uniform(ks[4], (DEPTH, GDN_HEADS), f32, 1.0, 16.0))
    dt = jnp.exp(jax.random.uniform(ks[5], (DEPTH, GDN_HEADS), f32, np.log(1e-3), np.log(1e-1)))
    gdn_dt_bias = dt + jnp.log(-jnp.expm1(-dt))
    gdn_norm_w = 1.0 + 0.02 * jax.random.normal(ks[6], (DEPTH, HEAD_DIM), f32)
    moba_q_norm_w = 1.0 + 0.02 * jax.random.normal(ks[7], (DEPTH, HEAD_DIM), f32)
    moba_k_norm_w = 1.0 + 0.02 * jax.random.normal(ks[8], (DEPTH, HEAD_DIM), f32)
    w_out_gdn = jax.random.normal(ks[9], (DEPTH, GDN_WIDTH, D_MODEL), f32) * GDN_WIDTH ** -0.5
    w_out_moba = jax.random.normal(ks[10], (DEPTH, MOBA_WIDTH, D_MODEL), f32) * MOBA_WIDTH ** -0.5
    w_o = jax.random.normal(ks[11], (DEPTH, D_MODEL, D_MODEL), f32) * D_MODEL ** -0.5
    return {"x": x, "norm_w": norm_w, "w_in": w_in, "gdn_conv_w": gdn_conv_w,
            "gdn_a_log": gdn_a_log, "gdn_dt_bias": gdn_dt_bias, "gdn_norm_w": gdn_norm_w,
            "moba_q_norm_w": moba_q_norm_w, "moba_k_norm_w": moba_k_norm_w,
            "w_out_gdn": w_out_gdn, "w_out_moba": w_out_moba, "w_o": w_o}


def reference(x, norm_w, w_in, gdn_conv_w, gdn_a_log, gdn_dt_bias, gdn_norm_w,
              moba_q_norm_w, moba_k_norm_w, w_out_gdn, w_out_moba, w_o):
    b_sz, t_len, _ = x.shape
    for l in range(DEPTH):
        h = rms_norm(x, norm_w[l])
        proj = jnp.einsum("btd,dc->btc", h, w_in[l])
        gdn_qkv, gdn_z, gdn_b, gdn_a, moba_qkv, moba_z, gates = jnp.split(proj, SPLIT_POINTS, axis=-1)

        qkv = jax.nn.silu(causal_depthwise_conv(gdn_qkv, gdn_conv_w[l]))
        qa, ka, va = jnp.split(qkv, 3, axis=-1)
        qa = l2_norm(qa.reshape(b_sz, t_len, GDN_HEADS, HEAD_DIM))
        ka = l2_norm(ka.reshape(b_sz, t_len, GDN_HEADS, HEAD_DIM))
        va = va.reshape(b_sz, t_len, GDN_HEADS, HEAD_DIM)
        beta = jax.nn.sigmoid(gdn_b.astype(jnp.float32))
        g = -jnp.exp(gdn_a_log[l].astype(jnp.float32)) * jax.nn.softplus(
            gdn_a.astype(jnp.float32) + gdn_dt_bias[l].astype(jnp.float32))
        oa = gated_delta_rule(qa, ka, va, g, beta)
        oa = rms_norm(oa, gdn_norm_w[l]).astype(x.dtype)
        oa = oa.reshape(b_sz, t_len, GDN_WIDTH) * jax.nn.silu(gdn_z)
        y_gdn = jnp.einsum("btc,cd->btd", oa, w_out_gdn[l])

        qb, kb, vb = jnp.split(moba_qkv, 3, axis=-1)
        qb = rope(rms_norm(qb.reshape(b_sz, t_len, MOBA_HEADS, HEAD_DIM), moba_q_norm_w[l]))
        kb = rope(rms_norm(kb.reshape(b_sz, t_len, MOBA_HEADS, HEAD_DIM), moba_k_norm_w[l]))
        vb = vb.reshape(b_sz, t_len, MOBA_HEADS, HEAD_DIM)
        ob = moba_attention(qb, kb, vb).reshape(b_sz, t_len, MOBA_WIDTH) * jax.nn.silu(moba_z)
        y_moba = jnp.einsum("btc,cd->btd", ob, w_out_moba[l])

        g_gdn, g_moba = jnp.split(gates, N_BRANCH, axis=-1)
        merged = jax.nn.sigmoid(g_gdn) * y_gdn + jax.nn.sigmoid(g_moba) * y_moba
        x = x + jnp.einsum("btd,de->bte", merged, w_o[l])
    return x
```

```python
import functools
import math

import jax
import jax.numpy as jnp
from jax import lax
from jax.experimental import pallas as pl
from jax.experimental.pallas import tpu as pltpu

F32 = jnp.float32
BF16 = jnp.bfloat16

EPS = 1e-6
ROPE_THETA = 10000.0
HEAD_DIM = 128
GDN_CONV = 4
GDN_CHUNK = 64
GDN_TILE = 256
MOBA_BLOCK = 256
MOBA_TOPK = 3
MASK_BIAS = -1e30
CARRY_ROWS = 8
VMEM_LIMIT = 56 * 1024 * 1024


def _mm(a, b):
    return jnp.dot(a.astype(BF16), b.astype(BF16), preferred_element_type=F32)


def _mm_nt(a, b):
    return lax.dot_general(a.astype(BF16), b.astype(BF16), (((1,), (1,)), ((), ())),
                           preferred_element_type=F32)


def _mm_tn(a, b):
    return lax.dot_general(a.astype(BF16), b.astype(BF16), (((0,), (0,)), ((), ())),
                           preferred_element_type=F32)


def _mm_exact(a, b):
    return jnp.dot(a, b, preferred_element_type=F32, precision=lax.Precision.HIGHEST)


def _sigmoid(x):
    return 1.0 / (1.0 + jnp.exp(-x))


def _rope_table_kernel(cos_ref, sin_ref):
    t_len, d = cos_ref.shape
    lane = lax.broadcasted_iota(jnp.int32, (t_len, d), 1)
    pos = lax.broadcasted_iota(jnp.int32, (t_len, d), 0).astype(F32)
    half = d // 2
    pair = jnp.where(lane < half, lane, lane - half).astype(F32)
    inv_freq = jnp.exp(pair * (-2.0 * math.log(ROPE_THETA) / d))
    ang = pos * inv_freq
    cos_ref[...] = jnp.cos(ang)
    sin_ref[...] = jnp.where(lane < half, -jnp.sin(ang), jnp.sin(ang))


def _rope_tables(t_len):
    return pl.pallas_call(
        _rope_table_kernel,
        out_shape=(jax.ShapeDtypeStruct((t_len, HEAD_DIM), F32),
                   jax.ShapeDtypeStruct((t_len, HEAD_DIM), F32)),
        name="rope_tables",
    )()


def _in_proj_kernel(x_ref, nw_ref, w_ref, wba_ref, o_ref, ba_ref, h_ref):
    @pl.when(pl.program_id(1) == 0)
    def _():
        x = x_ref[...]
        var = jnp.mean(x * x, axis=-1, keepdims=True)
        h = (x * lax.rsqrt(var + EPS) * nw_ref[...]).astype(BF16)
        h_ref[...] = h
        ba_ref[...] = jnp.dot(h, wba_ref[...], preferred_element_type=F32)

    o_ref[...] = jnp.dot(h_ref[...], w_ref[...], preferred_element_type=F32).astype(o_ref.dtype)


def _in_proj(x2, norm_w, w_main, w_ba, tm, tn):
    m, d = x2.shape
    n = w_main.shape[1]
    return pl.pallas_call(
        _in_proj_kernel,
        grid=(m // tm, n // tn),
        in_specs=[
            pl.BlockSpec((tm, d), lambda i, j: (i, 0)),
            pl.BlockSpec((1, d), lambda i, j: (0, 0)),
            pl.BlockSpec((d, tn), lambda i, j: (0, j)),
            pl.BlockSpec((d, HEAD_DIM), lambda i, j: (0, 0)),
        ],
        out_specs=(
            pl.BlockSpec((tm, tn), lambda i, j: (i, j)),
            pl.BlockSpec((tm, HEAD_DIM), lambda i, j: (i, 0)),
        ),
        out_shape=(jax.ShapeDtypeStruct((m, n), BF16),
                   jax.ShapeDtypeStruct((m, HEAD_DIM), F32)),
        scratch_shapes=[pltpu.VMEM((tm, d), BF16)],
        compiler_params=pltpu.CompilerParams(
            dimension_semantics=("parallel", "arbitrary"), vmem_limit_bytes=VMEM_LIMIT),
        name="in_proj",
    )(x2, norm_w, w_main, w_ba)


def _gdn_kernel(qkv_ref, z_ref, ba_ref, convw_ref, alog_ref, dtb_ref, nw_ref, o_ref,
                carry_ref, state_ref, q_s, k_s, v_s, *, n_heads):
    tt = qkv_ref.shape[0]
    c = GDN_CHUNK
    n_chunks = tt // c
    hd = HEAD_DIM

    @pl.when(pl.program_id(1) == 0)
    def _():
        carry_ref[...] = jnp.zeros_like(carry_ref)
        state_ref[...] = jnp.zeros_like(state_ref)

    for cb in range(3 * n_heads):
        cols = slice(cb * hd, (cb + 1) * hd)
        x32 = qkv_ref[:, cols].astype(F32)
        xe = jnp.concatenate([carry_ref[:, cols], x32], axis=0)
        wv = convw_ref[:, cols]
        y = x32 * wv[GDN_CONV - 1:GDN_CONV, :]
        for s in range(1, GDN_CONV):
            xr = pltpu.roll(xe, s, axis=0)[CARRY_ROWS:, :]
            y = y + xr * wv[GDN_CONV - 1 - s:GDN_CONV - s, :]
        carry_ref[:, cols] = x32[tt - CARRY_ROWS:, :]
        y = y * _sigmoid(y)
        sec, h = divmod(cb, n_heads)
        if sec < 2:
            y = y * lax.rsqrt(jnp.sum(y * y, axis=-1, keepdims=True) + EPS)
        (q_s, k_s, v_s)[sec][h] = y

    ba = ba_ref[...]
    beta_all = _sigmoid(ba)
    sp_in = ba + dtb_ref[...]
    softplus = jnp.maximum(sp_in, 0.0) + jnp.log1p(jnp.exp(-jnp.abs(sp_in)))
    g_all = -jnp.exp(alog_ref[...]) * softplus

    row = lax.broadcasted_iota(jnp.int32, (tt, tt), 0)
    col = lax.broadcasted_iota(jnp.int32, (tt, tt), 1)
    same_chunk = (row // c) == (col // c)
    causal = same_chunk & (col <= row)
    strict = same_chunk & (col < row)
    same16 = (row // 16) == (col // 16)
    same32 = (row // 32) == (col // 32)
    diag16 = strict & same16
    off16 = strict & same32 & jnp.logical_not(same16)
    off32 = strict & jnp.logical_not(same32)
    eye = (row == col).astype(F32)

    cg_all = _mm_exact(causal.astype(F32), g_all)
    cgl_all = _mm_exact(same_chunk.astype(F32), g_all)
    cgt_all = cg_all.T

    scale = hd ** -0.5
    nw = nw_ref[...]
    for h in range(n_heads):
        q = q_s[h] * scale
        k = k_s[h]
        v = v_s[h]
        beta = beta_all[:, h:h + 1]
        cg = cg_all[:, n_heads + h:n_heads + h + 1]
        cgl = cgl_all[:, n_heads + h:n_heads + h + 1]
        cg_row = cgt_all[n_heads + h:n_heads + h + 1, :]

        kb = k.astype(BF16)
        decay = jnp.exp(jnp.where(causal, cg - cg_row, -jnp.inf))
        a_mat = jnp.where(strict, beta * _mm_nt(kb, kb) * decay, 0.0)
        qk = _mm_nt(q, kb) * decay

        n1 = -jnp.where(diag16, a_mat, 0.0)
        inv = eye + n1
        n2 = _mm(n1, n1)
        inv = inv + _mm(inv, n2)
        n4 = _mm(n2, n2)
        inv = inv + _mm(inv, n4)
        n8 = _mm(n4, n4)
        inv = inv + _mm(inv, n8)
        inv = inv - _mm(_mm(inv, jnp.where(off16, a_mat, 0.0)), inv)
        inv = inv - _mm(_mm(inv, jnp.where(off32, a_mat, 0.0)), inv)

        ecg = jnp.exp(cg)
        rhs = jnp.concatenate([v * beta, k * (beta * ecg)], axis=1)
        uw = _mm(inv, rhs)
        u = uw[:, :hd]
        w = uw[:, hd:]
        q_dec = q * ecg
        k_dec = (k * jnp.exp(cgl - cg)).astype(BF16)
        g_last = jnp.exp(cgl)

        state = state_ref[h]
        v_new, o_state = [], []
        for ci in range(n_chunks):
            rs = slice(ci * c, (ci + 1) * c)
            r = _mm(jnp.concatenate([w[rs], q_dec[rs]], axis=0), state)
            vn = u[rs] - r[:c]
            o_state.append(r[c:])
            v_new.append(vn)
            state = state * g_last[ci * c:ci * c + 1, :] + _mm_tn(k_dec[rs], vn)
        state_ref[h] = state
        o = jnp.concatenate(o_state, axis=0) + _mm(qk, jnp.concatenate(v_new, axis=0))

        o = o * lax.rsqrt(jnp.mean(o * o, axis=-1, keepdims=True) + EPS) * nw
        zz = z_ref[:, h * hd:(h + 1) * hd].astype(F32)
        o_ref[:, h * hd:(h + 1) * hd] = (o * (zz * _sigmoid(zz))).astype(o_ref.dtype)


def _gdn(proj, ba, conv_w, alog_row, dtb_row, norm_w, b_sz, t_len, n_heads, z_block):
    width = n_heads * HEAD_DIM
    tt = min(GDN_TILE, t_len)
    n_t = t_len // tt
    return pl.pallas_call(
        functools.partial(_gdn_kernel, n_heads=n_heads),
        grid=(b_sz, n_t),
        in_specs=[
            pl.BlockSpec((tt, 3 * width), lambda b, t: (b * n_t + t, 0)),
            pl.BlockSpec((tt, width), lambda b, t: (b * n_t + t, z_block)),
            pl.BlockSpec((tt, HEAD_DIM), lambda b, t: (b * n_t + t, 0)),
            pl.BlockSpec((GDN_CONV, 3 * width), lambda b, t: (0, 0)),
            pl.BlockSpec((1, HEAD_DIM), lambda b, t: (0, 0)),
            pl.BlockSpec((1, HEAD_DIM), lambda b, t: (0, 0)),
            pl.BlockSpec((1, HEAD_DIM), lambda b, t: (0, 0)),
        ],
        out_specs=pl.BlockSpec((tt, width), lambda b, t: (b * n_t + t, 0)),
        out_shape=jax.ShapeDtypeStruct((b_sz * t_len, width), BF16),
        scratch_shapes=[
            pltpu.VMEM((CARRY_ROWS, 3 * width), F32),
            pltpu.VMEM((n_heads, HEAD_DIM, HEAD_DIM), F32),
            pltpu.VMEM((n_heads, tt, HEAD_DIM), F32),
            pltpu.VMEM((n_heads, tt, HEAD_DIM), F32),
            pltpu.VMEM((n_heads, tt, HEAD_DIM), F32),
        ],
        compiler_params=pltpu.CompilerParams(
            dimension_semantics=("parallel", "arbitrary"), vmem_limit_bytes=VMEM_LIMIT),
        name="gdn",
    )(proj, proj, ba, conv_w, alog_row, dtb_row, norm_w)


def _moba_kernel(q_ref, k_ref, v_ref, z_ref, cos_ref, sin_ref, qw_ref, kw_ref, o_ref,
                 qr_s, kaug_s, kmean_s):
    t_len, hd = q_ref.shape
    bs = MOBA_BLOCK
    n_blk = t_len // bs
    scale = hd ** -0.5

    def norm_rope(x_ref, w_ref, r0):
        x = x_ref[pl.ds(r0, bs), :].astype(F32)
        y = x * lax.rsqrt(jnp.mean(x * x, axis=-1, keepdims=True) + EPS) * w_ref[...]
        return (y * cos_ref[pl.ds(r0, bs), :]
                + pltpu.roll(y, hd // 2, axis=1) * sin_ref[pl.ds(r0, bs), :])

    kmean_s[...] = jnp.zeros_like(kmean_s)

    def prep(i, _):
        r0 = pl.multiple_of(i * bs, bs)
        qr_s[pl.ds(r0, bs), :] = norm_rope(q_ref, qw_ref, r0).astype(BF16)
        kr = norm_rope(k_ref, kw_ref, r0)
        lane = lax.broadcasted_iota(jnp.int32, (bs, hd), 1)
        kaug_s[pl.ds(r0, bs), :] = jnp.concatenate(
            [kr.astype(BF16), (lane == i).astype(BF16)], axis=1)
        kmean_s[pl.ds(i, 1), :] = jnp.mean(kr, axis=0, keepdims=True)
        return 0

    lax.fori_loop(0, n_blk, prep, 0)

    def q_block(i, _):
        r0 = pl.multiple_of(i * bs, bs)
        qi = qr_s[pl.ds(r0, bs), :]

        lane = lax.broadcasted_iota(jnp.int32, (bs, hd), 1)
        gate = jnp.where(lane < i, _mm_nt(qi, kmean_s[...]), -jnp.inf)
        sel = lane < 0
        for _ in range(MOBA_TOPK):
            best = jnp.max(gate, axis=-1, keepdims=True)
            is_best = (gate == best) & (gate > -jnp.inf)
            pick = lane == jnp.min(jnp.where(is_best, lane, hd), axis=-1, keepdims=True)
            sel = sel | pick
            gate = jnp.where(pick, -jnp.inf, gate)
        q_aug = jnp.concatenate([qi, jnp.where(sel, 0.0, MASK_BIAS).astype(BF16)], axis=1)

        row = lax.broadcasted_iota(jnp.int32, (bs, bs), 0)
        col = lax.broadcasted_iota(jnp.int32, (bs, bs), 1)
        s = _mm_nt(qi, kaug_s[pl.ds(r0, bs), :hd]) * scale
        s = jnp.where(col <= row, s, -jnp.inf)
        m0 = jnp.max(s, axis=-1, keepdims=True)
        p = jnp.exp(s - m0)
        l0 = jnp.sum(p, axis=-1, keepdims=True)
        acc0 = _mm(p, v_ref[pl.ds(r0, bs), :])

        def kv_block(j, carry):
            m_prev, l_prev, acc = carry
            c0 = pl.multiple_of(j * bs, bs)
            s = _mm_nt(q_aug, kaug_s[pl.ds(c0, bs), :]) * scale
            m_new = jnp.maximum(m_prev, jnp.max(s, axis=-1, keepdims=True))
            alpha = jnp.exp(m_prev - m_new)
            p = jnp.exp(s - m_new)
            l_new = alpha * l_prev + jnp.sum(p, axis=-1, keepdims=True)
            acc = alpha * acc + _mm(p, v_ref[pl.ds(c0, bs), :])
            return m_new, l_new, acc

        _, l_fin, acc = lax.fori_loop(0, i, kv_block, (m0, l0, acc0))
        zz = z_ref[pl.ds(r0, bs), :].astype(F32)
        o_ref[pl.ds(r0, bs), :] = (acc / l_fin * (zz * _sigmoid(zz))).astype(o_ref.dtype)
        return 0

    lax.fori_loop(0, n_blk, q_block, 0)


def _moba(proj, cos_t, sin_t, qw, kw, b_sz, t_len, n_heads, q_block0, z_block0):
    width = n_heads * HEAD_DIM

    def col(base):
        return lambda b, h: (b, base + h)

    return pl.pallas_call(
        _moba_kernel,
        grid=(b_sz, n_heads),
        in_specs=[
            pl.BlockSpec((t_len, HEAD_DIM), col(q_block0)),
            pl.BlockSpec((t_len, HEAD_DIM), col(q_block0 + n_heads)),
            pl.BlockSpec((t_len, HEAD_DIM), col(q_block0 + 2 * n_heads)),
            pl.BlockSpec((t_len, HEAD_DIM), col(z_block0)),
            pl.BlockSpec((t_len, HEAD_DIM), lambda b, h: (0, 0)),
            pl.BlockSpec((t_len, HEAD_DIM), lambda b, h: (0, 0)),
            pl.BlockSpec((1, HEAD_DIM), lambda b, h: (0, 0)),
            pl.BlockSpec((1, HEAD_DIM), lambda b, h: (0, 0)),
        ],
        out_specs=pl.BlockSpec((t_len, HEAD_DIM), col(0)),
        out_shape=jax.ShapeDtypeStruct((b_sz * t_len, width), BF16),
        scratch_shapes=[
            pltpu.VMEM((t_len, HEAD_DIM), BF16),
            pltpu.VMEM((t_len, 2 * HEAD_DIM), BF16),
            pltpu.VMEM((HEAD_DIM, HEAD_DIM), F32),
        ],
        compiler_params=pltpu.CompilerParams(
            dimension_semantics=("parallel", "parallel"), vmem_limit_bytes=VMEM_LIMIT),
        name="moba",
    )(proj, proj, proj, proj, cos_t, sin_t, qw, kw)


def _merge_kernel(x_ref, oa_ref, ob_ref, ga_ref, gb_ref, wa_ref, wb_ref, wo_ref, o_ref):
    ya = jnp.dot(oa_ref[...], wa_ref[...], preferred_element_type=F32)
    yb = jnp.dot(ob_ref[...], wb_ref[...], preferred_element_type=F32)
    merged = (_sigmoid(ga_ref[...].astype(F32)) * ya + _sigmoid(gb_ref[...].astype(F32)) * yb)
    o_ref[...] = x_ref[...] + jnp.dot(merged.astype(BF16), wo_ref[...],
                                      preferred_element_type=F32)


def _merge(x2, oa, ob, proj, wa, wb, wo, tm, gate_block0):
    m, d = x2.shape
    wa_rows, wb_rows = wa.shape[0], wb.shape[0]
    resident = dict(pipeline_mode=pl.Buffered(1))
    return pl.pallas_call(
        _merge_kernel,
        grid=(m // tm,),
        in_specs=[
            pl.BlockSpec((tm, d), lambda i: (i, 0)),
            pl.BlockSpec((tm, wa_rows), lambda i: (i, 0)),
            pl.BlockSpec((tm, wb_rows), lambda i: (i, 0)),
            pl.BlockSpec((tm, d), lambda i: (i, gate_block0)),
            pl.BlockSpec((tm, d), lambda i: (i, gate_block0 + 1)),
            pl.BlockSpec((wa_rows, d), lambda i: (0, 0), **resident),
            pl.BlockSpec((wb_rows, d), lambda i: (0, 0), **resident),
            pl.BlockSpec((d, d), lambda i: (0, 0), **resident),
        ],
        out_specs=pl.BlockSpec((tm, d), lambda i: (i, 0)),
        out_shape=jax.ShapeDtypeStruct((m, d), F32),
        compiler_params=pltpu.CompilerParams(
            dimension_semantics=("parallel",), vmem_limit_bytes=VMEM_LIMIT),
        name="merge_out",
    )(x2, oa, ob, proj, proj, wa, wb, wo)


def _lane_row(vec, offset):
    n = vec.shape[0]
    return jnp.pad(vec.astype(F32), (offset, HEAD_DIM - offset - n)).reshape(1, HEAD_DIM)


def _layer(x2, b_sz, t_len, cos_t, sin_t, norm_w, w_in, conv_w, a_log, dt_bias, gdn_norm_w,
           q_norm_w, k_norm_w, w_out_gdn, w_out_moba, w_o):
    m, d = x2.shape
    gw = w_out_gdn.shape[0]
    mw = w_out_moba.shape[0]
    gh = a_log.shape[0]
    mh = mw // HEAD_DIM
    assert gw == gh * HEAD_DIM and 2 * gh <= HEAD_DIM
    assert gw == mw and d % gw == 0 and (3 * gw) % gw == 0

    c0 = 3 * gw
    c1 = c0 + gw
    c2 = c1 + 2 * gh
    c3 = c2 + 3 * mw
    c4 = c3 + mw
    w_main = jnp.concatenate(
        [w_in[:, :c0], w_in[:, c2:c3], w_in[:, c0:c1], w_in[:, c3:c4], w_in[:, c4:]],
        axis=1).astype(BF16)
    w_ba = jnp.pad(w_in[:, c1:c2], ((0, 0), (0, HEAD_DIM - 2 * gh))).astype(BF16)
    gdn_z_block = 6
    moba_q_block = 3 * gw // HEAD_DIM
    moba_z_block = 7 * gw // HEAD_DIM
    gate_block = 8 * gw // d

    n_main = w_main.shape[1]
    tm = min(1024, m)
    tn = min(1024, n_main)
    proj, ba = _in_proj(x2, norm_w.reshape(1, d), w_main, w_ba, tm, tn)

    oa = _gdn(proj, ba, conv_w, _lane_row(a_log, gh), _lane_row(dt_bias, gh),
              gdn_norm_w.reshape(1, HEAD_DIM), b_sz, t_len, gh, gdn_z_block)
    ob = _moba(proj, cos_t, sin_t, q_norm_w.reshape(1, HEAD_DIM), k_norm_w.reshape(1, HEAD_DIM),
               b_sz, t_len, mh, moba_q_block, moba_z_block)
    return _merge(x2, oa, ob, proj, w_out_gdn.astype(BF16), w_out_moba.astype(BF16),
                  w_o.astype(BF16), min(256, m), gate_block)


def kernel(x, norm_w, w_in, gdn_conv_w, gdn_a_log, gdn_dt_bias, gdn_norm_w, moba_q_norm_w,
           moba_k_norm_w, w_out_gdn, w_out_moba, w_o):
    b_sz, t_len, d = x.shape
    cos_t, sin_t = _rope_tables(t_len)
    x2 = x.reshape(b_sz * t_len, d)
    for l in range(norm_w.shape[0]):
        x2 = _layer(x2, b_sz, t_len, cos_t, sin_t, norm_w[l], w_in[l], gdn_conv_w[l],
                    gdn_a_log[l], gdn_dt_bias[l], gdn_norm_w[l], moba_q_norm_w[l],
                    moba_k_norm_w[l], w_out_gdn[l], w_out_moba[l], w_o[l])
    return x2.reshape(b_sz, t_len, d)
```

```python
import functools
import math

import jax
import jax.numpy as jnp
from jax import lax
from jax.experimental import pallas as pl
from jax.experimental.pallas import tpu as pltpu

F32 = jnp.float32
BF16 = jnp.bfloat16

EPS = 1e-6
ROPE_THETA = 10000.0
HEAD_DIM = 128
GDN_CONV = 4
GDN_CHUNK = 64
GDN_TILE = 256
MOBA_BLOCK = 256
MOBA_TOPK = 3
MASK_BIAS = -1e30
CARRY_ROWS = 8
VMEM_LIMIT = 56 * 1024 * 1024


def _mm(a, b):
    return jnp.dot(a.astype(BF16), b.astype(BF16), preferred_element_type=F32)


def _mm_nt(a, b):
    return lax.dot_general(a.astype(BF16), b.astype(BF16), (((1,), (1,)), ((), ())),
                           preferred_element_type=F32)


def _mm_tn(a, b):
    return lax.dot_general(a.astype(BF16), b.astype(BF16), (((0,), (0,)), ((), ())),
                           preferred_element_type=F32)


def _mm_exact(a, b):
    return jnp.dot(a, b, preferred_element_type=F32, precision=lax.Precision.HIGHEST)


def _sigmoid(x):
    return 1.0 / (1.0 + jnp.exp(-x))


def _rope_table_kernel(cos_ref, sin_ref):
    t_len, d = cos_ref.shape
    lane = lax.broadcasted_iota(jnp.int32, (t_len, d), 1)
    pos = lax.broadcasted_iota(jnp.int32, (t_len, d), 0).astype(F32)
    half = d // 2
    pair = jnp.where(lane < half, lane, lane - half).astype(F32)
    inv_freq = jnp.exp(pair * (-2.0 * math.log(ROPE_THETA) / d))
    ang = pos * inv_freq
    cos_ref[...] = jnp.cos(ang)
    sin_ref[...] = jnp.where(lane < half, -jnp.sin(ang), jnp.sin(ang))


def _rope_tables(t_len):
    return pl.pallas_call(
        _rope_table_kernel,
        out_shape=(jax.ShapeDtypeStruct((t_len, HEAD_DIM), F32),
                   jax.ShapeDtypeStruct((t_len, HEAD_DIM), F32)),
        name="rope_tables",
    )()


def _in_proj_kernel(x_ref, nw_ref, w_ref, wba_ref, o_ref, ba_ref, h_ref):
    @pl.when(pl.program_id(1) == 0)
    def _():
        x = x_ref[...]
        var = jnp.mean(x * x, axis=-1, keepdims=True)
        h = (x * lax.rsqrt(var + EPS) * nw_ref[...]).astype(BF16)
        h_ref[...] = h
        ba_ref[...] = jnp.dot(h, wba_ref[...], preferred_element_type=F32)

    o_ref[...] = jnp.dot(h_ref[...], w_ref[...], preferred_element_type=F32).astype(o_ref.dtype)


def _in_proj(x2, norm_w, w_main, w_ba, tm, tn):
    m, d = x2.shape
    n = w_main.shape[1]
    return pl.pallas_call(
        _in_proj_kernel,
        grid=(m // tm, n // tn),
        in_specs=[
            pl.BlockSpec((tm, d), lambda i, j: (i, 0)),
            pl.BlockSpec((1, d), lambda i, j: (0, 0)),
            pl.BlockSpec((d, tn), lambda i, j: (0, j)),
            pl.BlockSpec((d, HEAD_DIM), lambda i, j: (0, 0)),
        ],
        out_specs=(
            pl.BlockSpec((tm, tn), lambda i, j: (i, j)),
            pl.BlockSpec((tm, HEAD_DIM), lambda i, j: (i, 0)),
        ),
        out_shape=(jax.ShapeDtypeStruct((m, n), BF16),
                   jax.ShapeDtypeStruct((m, HEAD_DIM), F32)),
        scratch_shapes=[pltpu.VMEM((tm, d), BF16)],
        compiler_params=pltpu.CompilerParams(
            dimension_semantics=("parallel", "arbitrary"), vmem_limit_bytes=VMEM_LIMIT),
        name="in_proj",
    )(x2, norm_w, w_main, w_ba)


def _gdn_kernel(qkv_ref, z_ref, ba_ref, convw_ref, alog_ref, dtb_ref, nw_ref, o_ref,
                carry_ref, state_ref, q_s, k_s, v_s, *, n_heads):
    tt = qkv_ref.shape[0]
    c = GDN_CHUNK
    n_chunks = tt // c
    hd = HEAD_DIM

    @pl.when(pl.program_id(1) == 0)
    def _():
        carry_ref[...] = jnp.zeros_like(carry_ref)
        state_ref[...] = jnp.zeros_like(state_ref)

    for cb in range(3 * n_heads):
        cols = slice(cb * hd, (cb + 1) * hd)
        x32 = qkv_ref[:, cols].astype(F32)
        xe = jnp.concatenate([carry_ref[:, cols], x32], axis=0)
        wv = convw_ref[:, cols]
        y = x32 * wv[GDN_CONV - 1:GDN_CONV, :]
        for s in range(1, GDN_CONV):
            xr = pltpu.roll(xe, s, axis=0)[CARRY_ROWS:, :]
            y = y + xr * wv[GDN_CONV - 1 - s:GDN_CONV - s, :]
        carry_ref[:, cols] = x32[tt - CARRY_ROWS:, :]
        y = y * _sigmoid(y)
        sec, h = divmod(cb, n_heads)
        if sec < 2:
            y = y * lax.rsqrt(jnp.sum(y * y, axis=-1, keepdims=True) + EPS)
        (q_s, k_s, v_s)[sec][h] = y

    ba = ba_ref[...]
    beta_all = _sigmoid(ba)
    sp_in = ba + dtb_ref[...]
    softplus = jnp.maximum(sp_in, 0.0) + jnp.log1p(jnp.exp(-jnp.abs(sp_in)))
    g_all = -jnp.exp(alog_ref[...]) * softplus

    row = lax.broadcasted_iota(jnp.int32, (tt, tt), 0)
    col = lax.broadcasted_iota(jnp.int32, (tt, tt), 1)
    same_chunk = (row // c) == (col // c)
    causal = same_chunk & (col <= row)
    strict = same_chunk & (col < row)
    same16 = (row // 16) == (col // 16)
    same32 = (row // 32) == (col // 32)
    diag16 = strict & same16
    off16 = strict & same32 & jnp.logical_not(same16)
    off32 = strict & jnp.logical_not(same32)
    eye = (row == col).astype(F32)

    cg_all = _mm_exact(causal.astype(F32), g_all)
    cgl_all = _mm_exact(same_chunk.astype(F32), g_all)
    cgt_all = cg_all.T

    scale = hd ** -0.5
    nw = nw_ref[...]
    heads = range(n_heads)

    def each(fn, *per_head):
        return [fn(*args) for args in zip(*per_head)]

    q = [q_s[h] * scale for h in heads]
    k = [k_s[h] for h in heads]
    v = [v_s[h] for h in heads]
    beta = [beta_all[:, h:h + 1] for h in heads]
    cg = [cg_all[:, n_heads + h:n_heads + h + 1] for h in heads]
    cgl = [cgl_all[:, n_heads + h:n_heads + h + 1] for h in heads]
    cg_row = [cgt_all[n_heads + h:n_heads + h + 1, :] for h in heads]

    kb = each(lambda x: x.astype(BF16), k)
    decay = each(lambda a, b: jnp.exp(jnp.where(causal, a - b, -jnp.inf)), cg, cg_row)
    kk = each(_mm_nt, kb, kb)
    qk = each(_mm_nt, q, kb)
    a_mat = each(lambda b, x, d: jnp.where(strict, b * x * d, 0.0), beta, kk, decay)
    qk = each(lambda x, d: x * d, qk, decay)

    n1 = each(lambda a: -jnp.where(diag16, a, 0.0), a_mat)
    inv = each(lambda n: eye + n, n1)
    power = n1
    for _ in range(3):
        power = each(_mm, power, power)
        inv = each(lambda t, p: t + _mm(t, p), inv, power)
    for off in (off16, off32):
        left = each(lambda t, a: _mm(t, jnp.where(off, a, 0.0)), inv, a_mat)
        inv = each(lambda t, x: t - _mm(x, t), inv, left)

    ecg = each(jnp.exp, cg)
    rhs = each(lambda vv, kx, b, e: jnp.concatenate([vv * b, kx * (b * e)], axis=1),
               v, k, beta, ecg)
    uw = each(_mm, inv, rhs)
    q_dec = each(lambda x, e: x * e, q, ecg)
    k_dec = each(lambda kx, a, b: (kx * jnp.exp(a - b)).astype(BF16), k, cgl, cg)
    g_last = each(jnp.exp, cgl)

    state = [state_ref[h] for h in heads]
    v_new = [[] for _ in heads]
    o_state = [[] for _ in heads]
    for ci in range(n_chunks):
        rs = slice(ci * c, (ci + 1) * c)
        r = each(lambda x, qd, s: _mm(jnp.concatenate([x[rs, hd:], qd[rs]], axis=0), s),
                 uw, q_dec, state)
        vn = each(lambda x, y: x[rs, :hd] - y[:c], uw, r)
        state = each(lambda s, g, kd, x: s * g[ci * c:ci * c + 1, :] + _mm_tn(kd[rs], x),
                     state, g_last, k_dec, vn)
        for h in heads:
            v_new[h].append(vn[h])
            o_state[h].append(r[h][c:])
    for h in heads:
        state_ref[h] = state[h]
    o = each(lambda os, x, vs: jnp.concatenate(os, axis=0) + _mm(x, jnp.concatenate(vs, axis=0)),
             o_state, qk, v_new)

    for h in heads:
        on = o[h] * lax.rsqrt(jnp.mean(o[h] * o[h], axis=-1, keepdims=True) + EPS) * nw
        zz = z_ref[:, h * hd:(h + 1) * hd].astype(F32)
        o_ref[:, h * hd:(h + 1) * hd] = (on * (zz * _sigmoid(zz))).astype(o_ref.dtype)


def _gdn(proj, ba, conv_w, alog_row, dtb_row, norm_w, b_sz, t_len, n_heads, z_block):
    width = n_heads * HEAD_DIM
    tt = min(GDN_TILE, t_len)
    n_t = t_len // tt
    return pl.pallas_call(
        functools.partial(_gdn_kernel, n_heads=n_heads),
        grid=(b_sz, n_t),
        in_specs=[
            pl.BlockSpec((tt, 3 * width), lambda b, t: (b * n_t + t, 0)),
            pl.BlockSpec((tt, width), lambda b, t: (b * n_t + t, z_block)),
            pl.BlockSpec((tt, HEAD_DIM), lambda b, t: (b * n_t + t, 0)),
            pl.BlockSpec((GDN_CONV, 3 * width), lambda b, t: (0, 0)),
            pl.BlockSpec((1, HEAD_DIM), lambda b, t: (0, 0)),
            pl.BlockSpec((1, HEAD_DIM), lambda b, t: (0, 0)),
            pl.BlockSpec((1, HEAD_DIM), lambda b, t: (0, 0)),
        ],
        out_specs=pl.BlockSpec((tt, width), lambda b, t: (b * n_t + t, 0)),
        out_shape=jax.ShapeDtypeStruct((b_sz * t_len, width), BF16),
        scratch_shapes=[
            pltpu.VMEM((CARRY_ROWS, 3 * width), F32),
            pltpu.VMEM((n_heads, HEAD_DIM, HEAD_DIM), F32),
            pltpu.VMEM((n_heads, tt, HEAD_DIM), F32),
            pltpu.VMEM((n_heads, tt, HEAD_DIM), F32),
            pltpu.VMEM((n_heads, tt, HEAD_DIM), F32),
        ],
        compiler_params=pltpu.CompilerParams(
            dimension_semantics=("parallel", "arbitrary"), vmem_limit_bytes=VMEM_LIMIT),
        name="gdn",
    )(proj, proj, ba, conv_w, alog_row, dtb_row, norm_w)


def _moba_kernel(pair_q_ref, pair_k_ref, q_ref, k_ref, v_ref, z_ref, cos_ref, sin_ref, qw_ref,
                 kw_ref, o_ref, qaug_s, kaug_s, vt_s, kmean_s, m_s, l_s, acc_s, *, unroll):
    t_len, hd = q_ref.shape
    bs = MOBA_BLOCK
    n_blk = t_len // bs
    n_pairs = pair_q_ref.shape[0]
    gate_rows = -(-n_blk // CARRY_ROWS) * CARRY_ROWS
    exp_scale = hd ** -0.5 * math.log2(math.e)

    def rows(i):
        return pl.ds(pl.multiple_of(i * bs, bs), bs)

    def norm_rope(x, w, cos, sin):
        x = x.astype(F32)
        y = x * lax.rsqrt(jnp.mean(x * x, axis=-1, keepdims=True) + EPS) * w
        return y * cos + pltpu.roll(y, hd // 2, axis=1) * sin


    def prep_load(i):
        return (i, q_ref[rows(i), :], k_ref[rows(i), :], v_ref[rows(i), :],
                cos_ref[rows(i), :], sin_ref[rows(i), :])

    def prep_compute(i, q, k, v, cos, sin):
        qr = norm_rope(q, qw_ref[...], cos, sin).astype(BF16)
        kr = norm_rope(k, kw_ref[...], cos, sin)
        lane = lax.broadcasted_iota(jnp.int32, (bs, hd), 1)
        k_aug = jnp.concatenate([kr.astype(BF16), (lane == i).astype(BF16)], axis=1)
        return qr, k_aug, jnp.mean(kr, axis=0, keepdims=True), v.astype(F32).T.astype(BF16)

    def prep_store(i, qr, k_aug, k_mean, v_t):
        qaug_s[rows(i), :hd] = qr
        kaug_s[rows(i), :] = k_aug
        kmean_s[pl.ds(i, 1), :] = k_mean
        vt_s[i] = v_t

    def own_load(i):
        return (i, qaug_s[rows(i), :hd], kaug_s[rows(i), :hd], vt_s[i], kmean_s[...])

    def own_compute(i, qi, ki, vi_t, k_mean):
        blk = lax.broadcasted_iota(jnp.int32, (gate_rows, bs), 0)
        gate = jnp.where(blk < i, _mm_nt(k_mean, qi)[:gate_rows], -jnp.inf)
        sel = blk < 0
        for _ in range(MOBA_TOPK):
            best = jnp.max(gate, axis=0, keepdims=True)
            is_best = (gate == best) & (gate > -jnp.inf)
            pick = blk == jnp.min(jnp.where(is_best, blk, hd), axis=0, keepdims=True)
            sel = sel | pick
            gate = jnp.where(pick, -jnp.inf, gate)
        bias_t = jnp.concatenate([jnp.where(sel, 0.0, MASK_BIAS),
                                  jnp.full((hd - gate_rows, bs), MASK_BIAS, F32)], axis=0)

        key = lax.broadcasted_iota(jnp.int32, (bs, bs), 0)
        qry = lax.broadcasted_iota(jnp.int32, (bs, bs), 1)
        s = jnp.where(key <= qry, _mm_nt(ki, qi), -jnp.inf)
        m0 = jnp.max(s, axis=0, keepdims=True)
        p = jnp.exp2((s - m0) * exp_scale)
        return bias_t.T.astype(BF16), m0, jnp.sum(p, axis=0, keepdims=True), _mm(vi_t, p)

    def own_store(i, bias, m0, l0, acc0):
        qaug_s[rows(i), hd:] = bias
        m_s[i] = m0
        l_s[i] = l0
        acc_s[i] = acc0

    n_groups = n_pairs // unroll

    def group(g):
        g = jnp.minimum(g, n_groups - 1)
        return [(pair_q_ref[g * unroll + u], pair_k_ref[g * unroll + u]) for u in range(unroll)]

    def weights(pairs):
        loaded = [(kaug_s[rows(j), :], qaug_s[rows(i), :], m_s[i]) for i, j in pairs]
        out = []
        for k_aug, q_aug, m_old in loaded:
            s = _mm_nt(k_aug, q_aug)
            m_new = jnp.maximum(m_old, jnp.max(s, axis=0, keepdims=True))
            p = jnp.exp2((s - m_new) * exp_scale)
            out.append((m_new, jnp.exp2((m_old - m_new) * exp_scale),
                        jnp.sum(p, axis=0, keepdims=True), p.astype(BF16)))
        for (i, _), w in zip(pairs, out):
            m_s[i] = w[0]
        return tuple(w[1:] for w in out)

    def past_body(t, w_now):
        now = group(t)
        state = [(vt_s[j], l_s[i], acc_s[i]) for i, j in now]
        folded = [(alpha * l_prev + p_sum, alpha * acc_prev + _mm(v_t, p))
                  for (v_t, l_prev, acc_prev), (alpha, p_sum, p) in zip(state, w_now)]
        w_ahead = weights(group(t + 1))
        for (i, _), (l_new, acc_new) in zip(now, folded):
            l_s[i] = l_new
            acc_s[i] = acc_new
        return w_ahead

    def finish_load(i):
        return (i, acc_s[i], l_s[i], z_ref[rows(i), :])

    def finish_compute(i, acc, l_fin, z):
        zz = z.astype(F32)
        return (((acc / l_fin).T * (zz * _sigmoid(zz))).astype(o_ref.dtype),)

    def finish_store(i, out):
        o_ref[rows(i), :] = out

    def run(load, compute, store, count):
        def body(t, _):
            loaded = [load(t * unroll + u) for u in range(unroll)]
            results = [compute(*vals) for vals in loaded]
            for vals, res in zip(loaded, results):
                store(vals[0], *res)
            return 0
        lax.fori_loop(0, count // unroll, body, 0)

    kmean_s[...] = jnp.zeros_like(kmean_s)
    run(prep_load, prep_compute, prep_store, n_blk)
    run(own_load, own_compute, own_store, n_blk)
    if n_pairs:
        lax.fori_loop(0, n_groups, past_body, weights(group(0)))
    run(finish_load, finish_compute, finish_store, n_blk)


def _moba_pair_schedule(n_blk):
    for unroll in (4, 2, 1):
        if n_blk % unroll:
            continue
        todo = {i: list(range(i)) for i in range(1, n_blk)}
        pairs = []
        while any(todo.values()):
            busiest = sorted((i for i in todo if todo[i]), key=lambda i: -len(todo[i]))[:unroll]
            if len(busiest) < unroll:
                break
            pairs += [(i, todo[i].pop()) for i in busiest]
        else:
            return unroll, pairs
    raise AssertionError("unroll == 1 always schedules")


def _moba(proj, cos_t, sin_t, qw, kw, b_sz, t_len, n_heads, q_block0, z_block0):
    width = n_heads * HEAD_DIM
    n_blk = t_len // MOBA_BLOCK
    unroll, pairs = _moba_pair_schedule(n_blk)
    pair_q = jnp.asarray([p[0] for p in pairs], jnp.int32)
    pair_k = jnp.asarray([p[1] for p in pairs], jnp.int32)

    def col(base):
        return lambda b, h, pq, pk: (b, base + h)

    def fixed(b, h, pq, pk):
        return (0, 0)

    return pl.pallas_call(
        functools.partial(_moba_kernel, unroll=unroll),
        grid_spec=pltpu.PrefetchScalarGridSpec(
            num_scalar_prefetch=2,
            grid=(b_sz, n_heads),
            in_specs=[
                pl.BlockSpec((t_len, HEAD_DIM), col(q_block0)),
                pl.BlockSpec((t_len, HEAD_DIM), col(q_block0 + n_heads)),
                pl.BlockSpec((t_len, HEAD_DIM), col(q_block0 + 2 * n_heads)),
                pl.BlockSpec((t_len, HEAD_DIM), col(z_block0)),
                pl.BlockSpec((t_len, HEAD_DIM), fixed),
                pl.BlockSpec((t_len, HEAD_DIM), fixed),
                pl.BlockSpec((1, HEAD_DIM), fixed),
                pl.BlockSpec((1, HEAD_DIM), fixed),
            ],
            out_specs=pl.BlockSpec((t_len, HEAD_DIM), col(0)),
            scratch_shapes=[
                pltpu.VMEM((t_len, 2 * HEAD_DIM), BF16),
                pltpu.VMEM((t_len, 2 * HEAD_DIM), BF16),
                pltpu.VMEM((n_blk, HEAD_DIM, MOBA_BLOCK), BF16),
                pltpu.VMEM((HEAD_DIM, HEAD_DIM), F32),
                pltpu.VMEM((n_blk, 1, MOBA_BLOCK), F32),
                pltpu.VMEM((n_blk, 1, MOBA_BLOCK), F32),
                pltpu.VMEM((n_blk, HEAD_DIM, MOBA_BLOCK), F32),
            ],
        ),
        out_shape=jax.ShapeDtypeStruct((b_sz * t_len, width), BF16),
        compiler_params=pltpu.CompilerParams(
            dimension_semantics=("parallel", "parallel"), vmem_limit_bytes=VMEM_LIMIT),
        name="moba",
    )(pair_q, pair_k, proj, proj, proj, proj, cos_t, sin_t, qw, kw)


def _merge_kernel(x_ref, oa_ref, ob_ref, ga_ref, gb_ref, wa_ref, wb_ref, wo_ref, o_ref):
    ya = jnp.dot(oa_ref[...], wa_ref[...], preferred_element_type=F32)
    yb = jnp.dot(ob_ref[...], wb_ref[...], preferred_element_type=F32)
    merged = (_sigmoid(ga_ref[...].astype(F32)) * ya + _sigmoid(gb_ref[...].astype(F32)) * yb)
    o_ref[...] = x_ref[...] + jnp.dot(merged.astype(BF16), wo_ref[...],
                                      preferred_element_type=F32)


def _merge(x2, oa, ob, proj, wa, wb, wo, tm, gate_block0):
    m, d = x2.shape
    wa_rows, wb_rows = wa.shape[0], wb.shape[0]
    resident = dict(pipeline_mode=pl.Buffered(1))
    return pl.pallas_call(
        _merge_kernel,
        grid=(m // tm,),
        in_specs=[
            pl.BlockSpec((tm, d), lambda i: (i, 0)),
            pl.BlockSpec((tm, wa_rows), lambda i: (i, 0)),
            pl.BlockSpec((tm, wb_rows), lambda i: (i, 0)),
            pl.BlockSpec((tm, d), lambda i: (i, gate_block0)),
            pl.BlockSpec((tm, d), lambda i: (i, gate_block0 + 1)),
            pl.BlockSpec((wa_rows, d), lambda i: (0, 0), **resident),
            pl.BlockSpec((wb_rows, d), lambda i: (0, 0), **resident),
            pl.BlockSpec((d, d), lambda i: (0, 0), **resident),
        ],
        out_specs=pl.BlockSpec((tm, d), lambda i: (i, 0)),
        out_shape=jax.ShapeDtypeStruct((m, d), F32),
        compiler_params=pltpu.CompilerParams(
            dimension_semantics=("parallel",), vmem_limit_bytes=VMEM_LIMIT),
        name="merge_out",
    )(x2, oa, ob, proj, proj, wa, wb, wo)


def _lane_row(vec, offset):
    n = vec.shape[0]
    return jnp.pad(vec.astype(F32), (offset, HEAD_DIM - offset - n)).reshape(1, HEAD_DIM)


def _layer(x2, b_sz, t_len, cos_t, sin_t, norm_w, w_in, conv_w, a_log, dt_bias, gdn_norm_w,
           q_norm_w, k_norm_w, w_out_gdn, w_out_moba, w_o):
    m, d = x2.shape
    gw = w_out_gdn.shape[0]
    mw = w_out_moba.shape[0]
    gh = a_log.shape[0]
    mh = mw // HEAD_DIM
    assert gw == gh * HEAD_DIM and 2 * gh <= HEAD_DIM
    assert gw == mw and (8 * gw) % d == 0

    c0 = 3 * gw
    c1 = c0 + gw
    c2 = c1 + 2 * gh
    c3 = c2 + 3 * mw
    c4 = c3 + mw
    w_main = jnp.concatenate(
        [w_in[:, :c0], w_in[:, c2:c3], w_in[:, c0:c1], w_in[:, c3:c4], w_in[:, c4:]],
        axis=1).astype(BF16)
    w_ba = jnp.pad(w_in[:, c1:c2], ((0, 0), (0, HEAD_DIM - 2 * gh))).astype(BF16)
    gdn_z_block = 6
    moba_q_block = 3 * gw // HEAD_DIM
    moba_z_block = 7 * gw // HEAD_DIM
    gate_block = 8 * gw // d

    n_main = w_main.shape[1]
    tm = min(1024, m)
    tn = min(1024, n_main)
    proj, ba = _in_proj(x2, norm_w.reshape(1, d), w_main, w_ba, tm, tn)

    oa = _gdn(proj, ba, conv_w, _lane_row(a_log, gh), _lane_row(dt_bias, gh),
              gdn_norm_w.reshape(1, HEAD_DIM), b_sz, t_len, gh, gdn_z_block)
    ob = _moba(proj, cos_t, sin_t, q_norm_w.reshape(1, HEAD_DIM), k_norm_w.reshape(1, HEAD_DIM),
               b_sz, t_len, mh, moba_q_block, moba_z_block)
    return _merge(x2, oa, ob, proj, w_out_gdn.astype(BF16), w_out_moba.astype(BF16),
                  w_o.astype(BF16), min(256, m), gate_block)


def kernel(x, norm_w, w_in, gdn_conv_w, gdn_a_log, gdn_dt_bias, gdn_norm_w, moba_q_norm_w,
           moba_k_norm_w, w_out_gdn, w_out_moba, w_o):
    b_sz, t_len, d = x.shape
    cos_t, sin_t = _rope_tables(t_len)
    x2 = x.reshape(b_sz * t_len, d)
    for l in range(norm_w.shape[0]):
        x2 = _layer(x2, b_sz, t_len, cos_t, sin_t, norm_w[l], w_in[l], gdn_conv_w[l],
                    gdn_a_log[l], gdn_dt_bias[l], gdn_norm_w[l], moba_q_norm_w[l],
                    moba_k_norm_w[l], w_out_gdn[l], w_out_moba[l], w_o[l])
    return x2.reshape(b_sz, t_len, d)
```

```python
import functools
import math

import jax
import jax.numpy as jnp
from jax import lax
from jax.experimental import pallas as pl
from jax.experimental.pallas import tpu as pltpu

F32 = jnp.float32
BF16 = jnp.bfloat16

EPS = 1e-6
ROPE_THETA = 10000.0
HEAD_DIM = 128
GDN_CONV = 4
GDN_CHUNK = 64
GDN_TILE = 256
MOBA_BLOCK = 256
MOBA_TOPK = 3
MASK_BIAS = -1e30
CARRY_ROWS = 8
SUM_ROWS = 16
VMEM_LIMIT = 56 * 1024 * 1024


def _mm(a, b):
    return jnp.dot(a.astype(BF16), b.astype(BF16), preferred_element_type=F32)


def _mm_nt(a, b):
    return lax.dot_general(a.astype(BF16), b.astype(BF16), (((1,), (1,)), ((), ())),
                           preferred_element_type=F32)


def _mm_tn(a, b):
    return lax.dot_general(a.astype(BF16), b.astype(BF16), (((0,), (0,)), ((), ())),
                           preferred_element_type=F32)


def _mm_exact(a, b):
    return jnp.dot(a, b, preferred_element_type=F32, precision=lax.Precision.HIGHEST)


def _sigmoid(x):
    return 0.5 * jnp.tanh(0.5 * x) + 0.5


def _silu(x):
    h = 0.5 * x
    return h + h * jnp.tanh(h)


def _rope_table_kernel(cos_ref, sin_ref):
    t_len, d = cos_ref.shape
    lane = lax.broadcasted_iota(jnp.int32, (t_len, d), 1)
    pos = lax.broadcasted_iota(jnp.int32, (t_len, d), 0).astype(F32)
    half = d // 2
    pair = jnp.where(lane < half, lane, lane - half).astype(F32)
    inv_freq = jnp.exp(pair * (-2.0 * math.log(ROPE_THETA) / d))
    ang = pos * inv_freq
    cos_ref[...] = jnp.cos(ang)
    sin_ref[...] = jnp.where(lane < half, -jnp.sin(ang), jnp.sin(ang))


def _rope_tables(t_len):
    return pl.pallas_call(
        _rope_table_kernel,
        out_shape=(jax.ShapeDtypeStruct((t_len, HEAD_DIM), F32),
                   jax.ShapeDtypeStruct((t_len, HEAD_DIM), F32)),
        name="rope_tables",
    )()


def _in_proj_kernel(x_ref, nw_ref, w_ref, wba_ref, o_ref, ba_ref, h_ref):
    @pl.when(pl.program_id(1) == 0)
    def _():
        x = x_ref[...]
        var = jnp.mean(x * x, axis=-1, keepdims=True)
        h = (x * lax.rsqrt(var + EPS) * nw_ref[...]).astype(BF16)
        h_ref[...] = h
        ba_ref[...] = jnp.dot(h, wba_ref[...], preferred_element_type=F32)

    o_ref[...] = jnp.dot(h_ref[...], w_ref[...], preferred_element_type=F32).astype(o_ref.dtype)


def _in_proj(x2, norm_w, w_main, w_ba, tm, tn):
    m, d = x2.shape
    n = w_main.shape[1]
    return pl.pallas_call(
        _in_proj_kernel,
        grid=(m // tm, n // tn),
        in_specs=[
            pl.BlockSpec((tm, d), lambda i, j: (i, 0)),
            pl.BlockSpec((1, d), lambda i, j: (0, 0)),
            pl.BlockSpec((d, tn), lambda i, j: (0, j)),
            pl.BlockSpec((d, HEAD_DIM), lambda i, j: (0, 0)),
        ],
        out_specs=(
            pl.BlockSpec((tm, tn), lambda i, j: (i, j)),
            pl.BlockSpec((tm, HEAD_DIM), lambda i, j: (i, 0)),
        ),
        out_shape=(jax.ShapeDtypeStruct((m, n), BF16),
                   jax.ShapeDtypeStruct((m, HEAD_DIM), F32)),
        scratch_shapes=[pltpu.VMEM((tm, d), BF16)],
        compiler_params=pltpu.CompilerParams(
            dimension_semantics=("parallel", "arbitrary"), vmem_limit_bytes=VMEM_LIMIT),
        name="in_proj",
    )(x2, norm_w, w_main, w_ba)


def _gdn_kernel(qkv_ref, z_ref, ba_ref, convw_ref, alog_ref, dtb_ref, nw_ref, o_ref,
                xwin_ref, state_ref, q_s, k_s, v_s, *, n_heads):
    tt = qkv_ref.shape[0]
    c = GDN_CHUNK
    n_chunks = tt // c
    hd = HEAD_DIM

    @pl.when(pl.program_id(1) == 0)
    def _():
        xwin_ref[...] = jnp.zeros_like(xwin_ref)
        state_ref[...] = jnp.zeros_like(state_ref)

    for cb in range(3 * n_heads):
        cols = slice(cb * hd, (cb + 1) * hd)
        xwin_ref[CARRY_ROWS:, cols] = qkv_ref[:, cols].astype(F32)
        wv = convw_ref[:, cols]
        y = xwin_ref[CARRY_ROWS:, cols] * wv[GDN_CONV - 1:GDN_CONV, :]
        for s in range(1, GDN_CONV):
            y = y + (xwin_ref[CARRY_ROWS - s:CARRY_ROWS - s + tt, cols]
                     * wv[GDN_CONV - 1 - s:GDN_CONV - s, :])
        xwin_ref[:CARRY_ROWS, cols] = xwin_ref[tt:, cols]
        y = _silu(y)
        sec, h = divmod(cb, n_heads)
        if sec < 2:
            y = y * lax.rsqrt(jnp.sum(y * y, axis=-1, keepdims=True) + EPS)
        (q_s, k_s, v_s)[sec][h] = y

    ba = ba_ref[...]
    beta_all = _sigmoid(ba)
    sp_in = ba + dtb_ref[...]
    softplus = jnp.maximum(sp_in, 0.0) + jnp.log1p(jnp.exp(-jnp.abs(sp_in)))
    g_all = -jnp.exp(alog_ref[...]) * softplus

    row = lax.broadcasted_iota(jnp.int32, (tt, tt), 0)
    col = lax.broadcasted_iota(jnp.int32, (tt, tt), 1)
    same_chunk = (row // c) == (col // c)
    block_diag = same_chunk.astype(BF16)

    cg_all = _mm_exact((same_chunk & (col <= row)).astype(F32), g_all)
    cgl_all = _mm_exact(same_chunk.astype(F32), g_all)
    cgt_all = cg_all.T

    prow = lax.broadcasted_iota(jnp.int32, (c, tt), 0)
    plane = lax.broadcasted_iota(jnp.int32, (c, tt), 1)
    pcol = plane % c
    in_chunk = [plane // c == ci for ci in range(n_chunks)]
    causal_p = pcol <= prow
    strict_p = pcol < prow
    same16 = (prow // 16) == (pcol // 16)
    same32 = (prow // 32) == (pcol // 32)
    diag16 = strict_p & same16
    off16 = strict_p & same32 & jnp.logical_not(same16)
    off32 = strict_p & jnp.logical_not(same32)
    eye_p = (pcol == prow).astype(F32)

    def pack(full):
        out = full[(n_chunks - 1) * c:]
        for ci in range(n_chunks - 2, -1, -1):
            out = jnp.where(in_chunk[ci], full[ci * c:(ci + 1) * c], out)
        return out

    def block_diag_of(packed):
        return jnp.concatenate([packed.astype(BF16)] * n_chunks, axis=0) * block_diag

    def mm_packed(packed, bd):
        return jnp.dot(packed.astype(BF16), bd, preferred_element_type=F32)

    scale = hd ** -0.5
    nw = nw_ref[...]
    heads = range(n_heads)

    def each(fn, *per_head):
        return [fn(*args) for args in zip(*per_head)]

    q = [q_s[h] * scale for h in heads]
    k = [k_s[h] for h in heads]
    v = [v_s[h] for h in heads]
    beta = [beta_all[:, h:h + 1] for h in heads]
    cg = [cg_all[:, n_heads + h:n_heads + h + 1] for h in heads]
    cgl = [cgl_all[:, n_heads + h:n_heads + h + 1] for h in heads]
    cg_row = [cgt_all[n_heads + h:n_heads + h + 1, :] for h in heads]

    kb = each(lambda x: x.astype(BF16), k)
    decay = each(lambda a, b: jnp.exp(jnp.where(causal_p, pack(a) - b, -jnp.inf)), cg, cg_row)
    kk = each(lambda x: pack(_mm_nt(x, x)), kb)
    qk = each(lambda x, y, d: block_diag_of(pack(_mm_nt(x, y)) * d), q, kb, decay)
    a_mat = each(lambda b, x, d: jnp.where(strict_p, pack(b) * x * d, 0.0), beta, kk, decay)

    n1 = each(lambda a: -jnp.where(diag16, a, 0.0), a_mat)
    inv = each(lambda n: eye_p + n, n1)
    power = n1
    power_bd = each(block_diag_of, power)
    for _ in range(3):
        power = each(mm_packed, power, power_bd)
        power_bd = each(block_diag_of, power)
        inv = each(lambda t, p: t + mm_packed(t, p), inv, power_bd)
    for off in (off16, off32):
        left = each(lambda t, a: mm_packed(t, block_diag_of(jnp.where(off, a, 0.0))), inv, a_mat)
        inv = each(lambda t, x: t - mm_packed(x, block_diag_of(t)), inv, left)

    ecg = each(jnp.exp, cg)
    rhs = each(lambda vv, kx, b, e: jnp.concatenate([vv * b, kx * (b * e)], axis=1),
               v, k, beta, ecg)
    uw = each(lambda t, x: jnp.dot(block_diag_of(t), x.astype(BF16), preferred_element_type=F32),
              inv, rhs)
    q_dec = each(lambda x, e: x * e, q, ecg)
    k_dec = each(lambda kx, a, b: (kx * jnp.exp(a - b)).astype(BF16), k, cgl, cg)
    g_last = each(jnp.exp, cgl)

    state = [state_ref[h] for h in heads]
    v_new = [[] for _ in heads]
    o_state = [[] for _ in heads]
    for ci in range(n_chunks):
        rs = slice(ci * c, (ci + 1) * c)
        r = each(lambda x, qd, s: _mm(jnp.concatenate([x[rs, hd:], qd[rs]], axis=0), s),
                 uw, q_dec, state)
        vn = each(lambda x, y: x[rs, :hd] - y[:c], uw, r)
        state = each(lambda s, g, kd, x: s * g[ci * c:ci * c + 1, :] + _mm_tn(kd[rs], x),
                     state, g_last, k_dec, vn)
        for h in heads:
            v_new[h].append(vn[h])
            o_state[h].append(r[h][c:])
    for h in heads:
        state_ref[h] = state[h]
    o = each(lambda os, x, vs: jnp.concatenate(os, axis=0) + _mm(x, jnp.concatenate(vs, axis=0)),
             o_state, qk, v_new)

    for h in heads:
        on = o[h] * lax.rsqrt(jnp.mean(o[h] * o[h], axis=-1, keepdims=True) + EPS) * nw
        zz = z_ref[:, h * hd:(h + 1) * hd].astype(F32)
        o_ref[:, h * hd:(h + 1) * hd] = (on * _silu(zz)).astype(o_ref.dtype)


def _gdn(proj, ba, conv_w, alog_row, dtb_row, norm_w, b_sz, t_len, n_heads, z_block):
    width = n_heads * HEAD_DIM
    tt = min(GDN_TILE, t_len)
    n_t = t_len // tt
    assert t_len % tt == 0 and tt % GDN_CHUNK == 0
    return pl.pallas_call(
        functools.partial(_gdn_kernel, n_heads=n_heads),
        grid=(b_sz, n_t),
        in_specs=[
            pl.BlockSpec((tt, 3 * width), lambda b, t: (b * n_t + t, 0)),
            pl.BlockSpec((tt, width), lambda b, t: (b * n_t + t, z_block)),
            pl.BlockSpec((tt, HEAD_DIM), lambda b, t: (b * n_t + t, 0)),
            pl.BlockSpec((GDN_CONV, 3 * width), lambda b, t: (0, 0)),
            pl.BlockSpec((1, HEAD_DIM), lambda b, t: (0, 0)),
            pl.BlockSpec((1, HEAD_DIM), lambda b, t: (0, 0)),
            pl.BlockSpec((1, HEAD_DIM), lambda b, t: (0, 0)),
        ],
        out_specs=pl.BlockSpec((tt, width), lambda b, t: (b * n_t + t, 0)),
        out_shape=jax.ShapeDtypeStruct((b_sz * t_len, width), BF16),
        scratch_shapes=[
            pltpu.VMEM((CARRY_ROWS + tt, 3 * width), F32),
            pltpu.VMEM((n_heads, HEAD_DIM, HEAD_DIM), F32),
            pltpu.VMEM((n_heads, tt, HEAD_DIM), F32),
            pltpu.VMEM((n_heads, tt, HEAD_DIM), F32),
            pltpu.VMEM((n_heads, tt, HEAD_DIM), F32),
        ],
        compiler_params=pltpu.CompilerParams(
            dimension_semantics=("parallel", "arbitrary"), vmem_limit_bytes=VMEM_LIMIT),
        name="gdn",
    )(proj, proj, ba, conv_w, alog_row, dtb_row, norm_w)


def _moba_kernel(pair_q_ref, pair_k_ref, q_ref, k_ref, v_ref, z_ref, cos_ref, sin_ref, qw_ref,
                 kw_ref, o_ref, qaug_s, kaug_s, vt_s, kmean_s, m_s, l_s, acc_s, *, unroll):
    t_len, hd = q_ref.shape
    bs = MOBA_BLOCK
    n_blk = t_len // bs
    n_pairs = pair_q_ref.shape[0]
    gate_rows = -(-n_blk // CARRY_ROWS) * CARRY_ROWS
    exp_scale = hd ** -0.5 * math.log2(math.e)

    def rows(i):
        return pl.ds(pl.multiple_of(i * bs, bs), bs)

    def norm_rope(x, w, cos, sin):
        x = x.astype(F32)
        y = x * lax.rsqrt(jnp.mean(x * x, axis=-1, keepdims=True) + EPS) * w
        return y * cos + pltpu.roll(y, hd // 2, axis=1) * sin

    def softmax_weights(s, m):
        return jnp.exp2(((s - m) * exp_scale).astype(BF16))


    def prep_load(i):
        return (i, q_ref[rows(i), :], k_ref[rows(i), :], v_ref[rows(i), :],
                cos_ref[rows(i), :], sin_ref[rows(i), :])

    def prep_compute(i, q, k, v, cos, sin):
        qr = norm_rope(q, qw_ref[...], cos, sin).astype(BF16)
        kr = norm_rope(k, kw_ref[...], cos, sin)
        lane = lax.broadcasted_iota(jnp.int32, (bs, hd), 1)
        k_aug = jnp.concatenate([kr.astype(BF16), (lane == i).astype(BF16)], axis=1)
        v_t = jnp.concatenate([v.astype(F32).T, jnp.ones((SUM_ROWS, bs), F32)], axis=0)
        return qr, k_aug, jnp.mean(kr, axis=0, keepdims=True), v_t.astype(BF16)

    def prep_store(i, qr, k_aug, k_mean, v_t):
        qaug_s[rows(i), :hd] = qr
        kaug_s[rows(i), :] = k_aug
        kmean_s[pl.ds(i, 1), :] = k_mean
        vt_s[i] = v_t

    def own_load(i):
        return (i, qaug_s[rows(i), :hd], kaug_s[rows(i), :hd], vt_s[i], kmean_s[...])

    def own_compute(i, qi, ki, vi_t, k_mean):
        blk = lax.broadcasted_iota(jnp.int32, (gate_rows, bs), 0)
        gate = jnp.where(blk < i, _mm_nt(k_mean, qi)[:gate_rows], -jnp.inf)
        sel = blk < 0
        for _ in range(MOBA_TOPK):
            best = jnp.max(gate, axis=0, keepdims=True)
            is_best = (gate == best) & (gate > -jnp.inf)
            pick = blk == jnp.min(jnp.where(is_best, blk, hd), axis=0, keepdims=True)
            sel = sel | pick
            gate = jnp.where(pick, -jnp.inf, gate)
        bias_t = jnp.concatenate([jnp.where(sel, 0.0, MASK_BIAS),
                                  jnp.full((hd - gate_rows, bs), MASK_BIAS, F32)], axis=0)

        key = lax.broadcasted_iota(jnp.int32, (bs, bs), 0)
        qry = lax.broadcasted_iota(jnp.int32, (bs, bs), 1)
        s = jnp.where(key <= qry, _mm_nt(ki, qi), -jnp.inf)
        m0 = jnp.max(s, axis=0, keepdims=True)
        p = softmax_weights(s, m0)
        l0 = jnp.sum(p.astype(F32), axis=0, keepdims=True)
        return bias_t.T.astype(BF16), m0, l0, jnp.dot(vi_t[:hd], p, preferred_element_type=F32)

    def own_store(i, bias, m0, l0, acc0):
        qaug_s[rows(i), hd:] = bias
        m_s[i] = m0
        l_s[i] = l0
        acc_s[i] = acc0

    n_groups = n_pairs // unroll

    def group(g):
        g = jnp.minimum(g, n_groups - 1)
        return [(pair_q_ref[g * unroll + u], pair_k_ref[g * unroll + u]) for u in range(unroll)]

    def weights(pairs):
        loaded = [(kaug_s[rows(j), :], qaug_s[rows(i), :], m_s[i]) for i, j in pairs]
        out = []
        for k_aug, q_aug, m_old in loaded:
            s = _mm_nt(k_aug, q_aug)
            m_new = jnp.maximum(m_old, jnp.max(s, axis=0, keepdims=True))
            out.append((m_new, jnp.exp2((m_old - m_new) * exp_scale), softmax_weights(s, m_new)))
        for (i, _), w in zip(pairs, out):
            m_s[i] = w[0]
        return tuple(w[1:] for w in out)

    def past_body(t, w_now):
        now = group(t)
        state = [(vt_s[j], l_s[i], acc_s[i]) for i, j in now]
        folded = []
        for (v_t, l_prev, acc_prev), (alpha, p) in zip(state, w_now):
            pv = jnp.dot(v_t, p, preferred_element_type=F32)
            folded.append((alpha * l_prev + pv[hd:hd + 1], alpha * acc_prev + pv[:hd]))
        w_ahead = weights(group(t + 1))
        for (i, _), (l_new, acc_new) in zip(now, folded):
            l_s[i] = l_new
            acc_s[i] = acc_new
        return w_ahead

    def finish_load(i):
        return (i, acc_s[i], l_s[i], z_ref[rows(i), :])

    def finish_compute(i, acc, l_fin, z):
        zz = z.astype(F32)
        return (((acc / l_fin).T * _silu(zz)).astype(o_ref.dtype),)

    def finish_store(i, out):
        o_ref[rows(i), :] = out

    def run(load, compute, store, count):
        def body(t, _):
            loaded = [load(t * unroll + u) for u in range(unroll)]
            results = [compute(*vals) for vals in loaded]
            for vals, res in zip(loaded, results):
                store(vals[0], *res)
            return 0
        lax.fori_loop(0, count // unroll, body, 0)

    kmean_s[...] = jnp.zeros_like(kmean_s)
    run(prep_load, prep_compute, prep_store, n_blk)
    run(own_load, own_compute, own_store, n_blk)
    if n_pairs:
        lax.fori_loop(0, n_groups, past_body, weights(group(0)))
    run(finish_load, finish_compute, finish_store, n_blk)


def _moba_pair_schedule(n_blk):
    for unroll in (8, 4, 2, 1):
        if n_blk % unroll:
            continue
        todo = {i: list(range(i)) for i in range(1, n_blk)}
        pairs = []
        while any(todo.values()):
            busiest = sorted((i for i in todo if todo[i]), key=lambda i: -len(todo[i]))[:unroll]
            if len(busiest) < unroll:
                break
            pairs += [(i, todo[i].pop()) for i in busiest]
        else:
            return unroll, pairs
    raise AssertionError("unroll == 1 always schedules")


def _moba(proj, cos_t, sin_t, qw, kw, b_sz, t_len, n_heads, q_block0, z_block0):
    width = n_heads * HEAD_DIM
    n_blk = t_len // MOBA_BLOCK
    unroll, pairs = _moba_pair_schedule(n_blk)
    pair_q = jnp.asarray([p[0] for p in pairs], jnp.int32)
    pair_k = jnp.asarray([p[1] for p in pairs], jnp.int32)

    def col(base):
        return lambda b, h, pq, pk: (b, base + h)

    def fixed(b, h, pq, pk):
        return (0, 0)

    return pl.pallas_call(
        functools.partial(_moba_kernel, unroll=unroll),
        grid_spec=pltpu.PrefetchScalarGridSpec(
            num_scalar_prefetch=2,
            grid=(b_sz, n_heads),
            in_specs=[
                pl.BlockSpec((t_len, HEAD_DIM), col(q_block0)),
                pl.BlockSpec((t_len, HEAD_DIM), col(q_block0 + n_heads)),
                pl.BlockSpec((t_len, HEAD_DIM), col(q_block0 + 2 * n_heads)),
                pl.BlockSpec((t_len, HEAD_DIM), col(z_block0)),
                pl.BlockSpec((t_len, HEAD_DIM), fixed),
                pl.BlockSpec((t_len, HEAD_DIM), fixed),
                pl.BlockSpec((1, HEAD_DIM), fixed),
                pl.BlockSpec((1, HEAD_DIM), fixed),
            ],
            out_specs=pl.BlockSpec((t_len, HEAD_DIM), col(0)),
            scratch_shapes=[
                pltpu.VMEM((t_len, 2 * HEAD_DIM), BF16),
                pltpu.VMEM((t_len, 2 * HEAD_DIM), BF16),
                pltpu.VMEM((n_blk, HEAD_DIM + SUM_ROWS, MOBA_BLOCK), BF16),
                pltpu.VMEM((HEAD_DIM, HEAD_DIM), F32),
                pltpu.VMEM((n_blk, 1, MOBA_BLOCK), F32),
                pltpu.VMEM((n_blk, 1, MOBA_BLOCK), F32),
                pltpu.VMEM((n_blk, HEAD_DIM, MOBA_BLOCK), F32),
            ],
        ),
        out_shape=jax.ShapeDtypeStruct((b_sz * t_len, width), BF16),
        compiler_params=pltpu.CompilerParams(
            dimension_semantics=("parallel", "parallel"), vmem_limit_bytes=VMEM_LIMIT),
        name="moba",
    )(pair_q, pair_k, proj, proj, proj, proj, cos_t, sin_t, qw, kw)


def _merge_kernel(x_ref, oa_ref, ob_ref, ga_ref, gb_ref, wa_ref, wb_ref, wo_ref, o_ref):
    ya = jnp.dot(oa_ref[...], wa_ref[...], preferred_element_type=F32)
    yb = jnp.dot(ob_ref[...], wb_ref[...], preferred_element_type=F32)
    merged = (_sigmoid(ga_ref[...].astype(F32)) * ya + _sigmoid(gb_ref[...].astype(F32)) * yb)
    o_ref[...] = x_ref[...] + jnp.dot(merged.astype(BF16), wo_ref[...],
                                      preferred_element_type=F32)


def _merge(x2, oa, ob, proj, wa, wb, wo, tm, gate_block0):
    m, d = x2.shape
    wa_rows, wb_rows = wa.shape[0], wb.shape[0]
    resident = dict(pipeline_mode=pl.Buffered(1))
    return pl.pallas_call(
        _merge_kernel,
        grid=(m // tm,),
        in_specs=[
            pl.BlockSpec((tm, d), lambda i: (i, 0)),
            pl.BlockSpec((tm, wa_rows), lambda i: (i, 0)),
            pl.BlockSpec((tm, wb_rows), lambda i: (i, 0)),
            pl.BlockSpec((tm, d), lambda i: (i, gate_block0)),
            pl.BlockSpec((tm, d), lambda i: (i, gate_block0 + 1)),
            pl.BlockSpec((wa_rows, d), lambda i: (0, 0), **resident),
            pl.BlockSpec((wb_rows, d), lambda i: (0, 0), **resident),
            pl.BlockSpec((d, d), lambda i: (0, 0), **resident),
        ],
        out_specs=pl.BlockSpec((tm, d), lambda i: (i, 0)),
        out_shape=jax.ShapeDtypeStruct((m, d), F32),
        compiler_params=pltpu.CompilerParams(
            dimension_semantics=("parallel",), vmem_limit_bytes=VMEM_LIMIT),
        name="merge_out",
    )(x2, oa, ob, proj, proj, wa, wb, wo)


def _lane_row(vec, offset):
    n = vec.shape[0]
    return jnp.pad(vec.astype(F32), (offset, HEAD_DIM - offset - n)).reshape(1, HEAD_DIM)


def _layer(x2, b_sz, t_len, cos_t, sin_t, norm_w, w_in, conv_w, a_log, dt_bias, gdn_norm_w,
           q_norm_w, k_norm_w, w_out_gdn, w_out_moba, w_o):
    m, d = x2.shape
    gw = w_out_gdn.shape[0]
    mw = w_out_moba.shape[0]
    gh = a_log.shape[0]
    mh = mw // HEAD_DIM
    assert gw == gh * HEAD_DIM and 2 * gh <= HEAD_DIM
    assert gw == mw and (8 * gw) % d == 0

    c0 = 3 * gw
    c1 = c0 + gw
    c2 = c1 + 2 * gh
    c3 = c2 + 3 * mw
    c4 = c3 + mw
    w_main = jnp.concatenate(
        [w_in[:, :c0], w_in[:, c2:c3], w_in[:, c0:c1], w_in[:, c3:c4], w_in[:, c4:]],
        axis=1).astype(BF16)
    w_ba = jnp.pad(w_in[:, c1:c2], ((0, 0), (0, HEAD_DIM - 2 * gh))).astype(BF16)
    gdn_z_block = 6
    moba_q_block = 3 * gw // HEAD_DIM
    moba_z_block = 7 * gw // HEAD_DIM
    gate_block = 8 * gw // d

    n_main = w_main.shape[1]
    tm = min(1024, m)
    tn = min(1024, n_main)
    proj, ba = _in_proj(x2, norm_w.reshape(1, d), w_main, w_ba, tm, tn)

    oa = _gdn(proj, ba, conv_w, _lane_row(a_log, gh), _lane_row(dt_bias, gh),
              gdn_norm_w.reshape(1, HEAD_DIM), b_sz, t_len, gh, gdn_z_block)
    ob = _moba(proj, cos_t, sin_t, q_norm_w.reshape(1, HEAD_DIM), k_norm_w.reshape(1, HEAD_DIM),
               b_sz, t_len, mh, moba_q_block, moba_z_block)
    return _merge(x2, oa, ob, proj, w_out_gdn.astype(BF16), w_out_moba.astype(BF16),
                  w_o.astype(BF16), min(256, m), gate_block)


def kernel(x, norm_w, w_in, gdn_conv_w, gdn_a_log, gdn_dt_bias, gdn_norm_w, moba_q_norm_w,
           moba_k_norm_w, w_out_gdn, w_out_moba, w_o):
    b_sz, t_len, d = x.shape
    cos_t, sin_t = _rope_tables(t_len)
    x2 = x.reshape(b_sz * t_len, d)
    for l in range(norm_w.shape[0]):
        x2 = _layer(x2, b_sz, t_len, cos_t, sin_t, norm_w[l], w_in[l], gdn_conv_w[l],
                    gdn_a_log[l], gdn_dt_bias[l], gdn_norm_w[l], moba_q_norm_w[l],
                    moba_k_norm_w[l], w_out_gdn[l], w_out_moba[l], w_o[l])
    return x2.reshape(b_sz, t_len, d)
```

```python
import functools
import math

import jax
import jax.numpy as jnp
from jax import lax
from jax.experimental import pallas as pl
from jax.experimental.pallas import tpu as pltpu

F32 = jnp.float32
BF16 = jnp.bfloat16

EPS = 1e-6
ROPE_THETA = 10000.0
HEAD_DIM = 128
GDN_CONV = 4
GDN_CHUNK = 64
GDN_TILE = 256
MOBA_BLOCK = 256
MOBA_TOPK = 3
MASK_BIAS = -1e30
CARRY_ROWS = 8
SUM_ROWS = 16
VMEM_LIMIT = 56 * 1024 * 1024


def _mm(a, b):
    return jnp.dot(a.astype(BF16), b.astype(BF16), preferred_element_type=F32)


def _mm_nt(a, b):
    return lax.dot_general(a.astype(BF16), b.astype(BF16), (((1,), (1,)), ((), ())),
                           preferred_element_type=F32)


def _mm_tn(a, b):
    return lax.dot_general(a.astype(BF16), b.astype(BF16), (((0,), (0,)), ((), ())),
                           preferred_element_type=F32)


def _mm_exact(a, b):
    return jnp.dot(a, b, preferred_element_type=F32, precision=lax.Precision.HIGHEST)


def _sigmoid(x):
    return 0.5 * jnp.tanh(0.5 * x) + 0.5


def _silu(x):
    h = 0.5 * x
    return h + h * jnp.tanh(h)


def _rope_table_kernel(cos_ref, sin_ref):
    t_len, d = cos_ref.shape
    lane = lax.broadcasted_iota(jnp.int32, (t_len, d), 1)
    pos = lax.broadcasted_iota(jnp.int32, (t_len, d), 0).astype(F32)
    half = d // 2
    pair = jnp.where(lane < half, lane, lane - half).astype(F32)
    inv_freq = jnp.exp(pair * (-2.0 * math.log(ROPE_THETA) / d))
    ang = pos * inv_freq
    cos_ref[...] = jnp.cos(ang)
    sin_ref[...] = jnp.where(lane < half, -jnp.sin(ang), jnp.sin(ang))


def _rope_tables(t_len):
    return pl.pallas_call(
        _rope_table_kernel,
        out_shape=(jax.ShapeDtypeStruct((t_len, HEAD_DIM), F32),
                   jax.ShapeDtypeStruct((t_len, HEAD_DIM), F32)),
        name="rope_tables",
    )()


_TILE_GDN_V = 2
_TILE_MOBA_Q = 4
_TILE_MOBA_K = 5
EPILOGUE_ROWS = 256


def _in_proj_kernel(x_ref, nw_ref, w_ref, wba_ref, convw_ref, cos_ref, sin_ref, qkw_ref,
                    o_ref, ba_ref, h_ref, acc_ref, carry_ref, *, rows_per_seq):
    i = pl.program_id(0)
    j = pl.program_id(1)
    tm, tn = o_ref.shape
    hd = HEAD_DIM
    slab = min(EPILOGUE_ROWS, tm)

    @pl.when((i == 0) & (j == 0))
    def _():
        carry_ref[...] = jnp.zeros_like(carry_ref)

    @pl.when(j == 0)
    def _():
        x = x_ref[...]
        var = jnp.mean(x * x, axis=-1, keepdims=True)
        h = (x * lax.rsqrt(var + EPS) * nw_ref[...]).astype(BF16)
        h_ref[...] = h
        ba_ref[...] = jnp.dot(h, wba_ref[...], preferred_element_type=F32)

    def project(r0):
        return jnp.dot(h_ref[r0:r0 + slab, :], w_ref[...], preferred_element_type=F32)

    @pl.when(j <= _TILE_GDN_V)
    def _():
        seq_start = (i * tm) % rows_per_seq == 0
        acc_ref[:CARRY_ROWS, :] = jnp.where(seq_start, 0.0, carry_ref[j])
        wv = convw_ref[...]
        for r0 in range(0, tm, slab):
            acc_ref[CARRY_ROWS + r0:CARRY_ROWS + r0 + slab, :] = project(r0)
            y = acc_ref[CARRY_ROWS + r0:CARRY_ROWS + r0 + slab, :] * wv[GDN_CONV - 1:GDN_CONV, :]
            for s in range(1, GDN_CONV):
                y = y + (acc_ref[CARRY_ROWS + r0 - s:CARRY_ROWS + r0 - s + slab, :]
                         * wv[GDN_CONV - 1 - s:GDN_CONV - s, :])
            y = _silu(y)
            for h in range(tn // hd):
                yh = y[:, h * hd:(h + 1) * hd]
                l2 = lax.rsqrt(jnp.sum(yh * yh, axis=-1, keepdims=True) + EPS)
                yh = yh * jnp.where(j < _TILE_GDN_V, l2, 1.0)
                o_ref[r0:r0 + slab, h * hd:(h + 1) * hd] = yh.astype(o_ref.dtype)
        carry_ref[j] = acc_ref[tm:, :]

    @pl.when((j == _TILE_MOBA_Q) | (j == _TILE_MOBA_K))
    def _():
        w = jnp.where(j == _TILE_MOBA_Q, qkw_ref[0:1, :], qkw_ref[1:2, :])
        for r0 in range(0, tm, slab):
            acc = project(r0)
            cos = cos_ref[r0:r0 + slab, :]
            sin = sin_ref[r0:r0 + slab, :]
            for h in range(tn // hd):
                x = acc[:, h * hd:(h + 1) * hd]
                y = x * lax.rsqrt(jnp.mean(x * x, axis=-1, keepdims=True) + EPS) * w
                y = y * cos + pltpu.roll(y, hd // 2, axis=1) * sin
                o_ref[r0:r0 + slab, h * hd:(h + 1) * hd] = y.astype(o_ref.dtype)

    @pl.when((j > _TILE_GDN_V) & (j != _TILE_MOBA_Q) & (j != _TILE_MOBA_K))
    def _():
        for r0 in range(0, tm, slab):
            o_ref[r0:r0 + slab, :] = project(r0).astype(o_ref.dtype)


def _in_proj(x2, norm_w, w_main, w_ba, conv_w, cos_t, sin_t, qk_norm_w, tm, tn, t_len):
    m, d = x2.shape
    n = w_main.shape[1]
    assert t_len % tm == 0 and conv_w.shape[1] == (_TILE_GDN_V + 1) * tn
    seq_tiles = t_len // tm
    return pl.pallas_call(
        functools.partial(_in_proj_kernel, rows_per_seq=t_len),
        grid=(m // tm, n // tn),
        in_specs=[
            pl.BlockSpec((tm, d), lambda i, j: (i, 0)),
            pl.BlockSpec((1, d), lambda i, j: (0, 0)),
            pl.BlockSpec((d, tn), lambda i, j: (0, j)),
            pl.BlockSpec((d, HEAD_DIM), lambda i, j: (0, 0)),
            pl.BlockSpec((GDN_CONV, tn), lambda i, j: (0, jnp.minimum(j, _TILE_GDN_V))),
            pl.BlockSpec((tm, HEAD_DIM), lambda i, j: (i % seq_tiles, 0)),
            pl.BlockSpec((tm, HEAD_DIM), lambda i, j: (i % seq_tiles, 0)),
            pl.BlockSpec((2, HEAD_DIM), lambda i, j: (0, 0)),
        ],
        out_specs=(
            pl.BlockSpec((tm, tn), lambda i, j: (i, j)),
            pl.BlockSpec((tm, HEAD_DIM), lambda i, j: (i, 0)),
        ),
        out_shape=(jax.ShapeDtypeStruct((m, n), BF16),
                   jax.ShapeDtypeStruct((m, HEAD_DIM), F32)),
        scratch_shapes=[
            pltpu.VMEM((tm, d), BF16),
            pltpu.VMEM((CARRY_ROWS + tm, tn), F32),
            pltpu.VMEM((_TILE_GDN_V + 1, CARRY_ROWS, tn), F32),
        ],
        compiler_params=pltpu.CompilerParams(
            dimension_semantics=("arbitrary", "arbitrary"), vmem_limit_bytes=VMEM_LIMIT),
        name="in_proj",
    )(x2, norm_w, w_main, w_ba, conv_w, cos_t, sin_t, qk_norm_w)


def _gdn_kernel(qkv_ref, z_ref, ba_ref, alog_ref, dtb_ref, nw_ref, o_ref, state_ref, *, n_heads):
    tt = qkv_ref.shape[0]
    c = GDN_CHUNK
    n_chunks = tt // c
    hd = HEAD_DIM
    width = n_heads * hd

    @pl.when(pl.program_id(1) == 0)
    def _():
        state_ref[...] = jnp.zeros_like(state_ref)

    ba = ba_ref[...]
    beta_all = _sigmoid(ba)
    sp_in = ba + dtb_ref[...]
    softplus = jnp.maximum(sp_in, 0.0) + jnp.log1p(jnp.exp(-jnp.abs(sp_in)))
    g_all = -jnp.exp(alog_ref[...]) * softplus

    row = lax.broadcasted_iota(jnp.int32, (tt, tt), 0)
    col = lax.broadcasted_iota(jnp.int32, (tt, tt), 1)
    same_chunk = (row // c) == (col // c)
    block_diag = same_chunk.astype(BF16)

    cg_all = _mm_exact((same_chunk & (col <= row)).astype(F32), g_all)
    cgl_all = _mm_exact(same_chunk.astype(F32), g_all)
    cgt_all = cg_all.T

    prow = lax.broadcasted_iota(jnp.int32, (c, tt), 0)
    plane = lax.broadcasted_iota(jnp.int32, (c, tt), 1)
    pcol = plane % c
    in_chunk = [plane // c == ci for ci in range(n_chunks)]
    causal_p = pcol <= prow
    strict_p = pcol < prow
    same16 = (prow // 16) == (pcol // 16)
    same32 = (prow // 32) == (pcol // 32)
    diag16 = strict_p & same16
    off16 = strict_p & same32 & jnp.logical_not(same16)
    off32 = strict_p & jnp.logical_not(same32)
    eye_p = (pcol == prow).astype(F32)

    def pack(full):
        out = full[(n_chunks - 1) * c:]
        for ci in range(n_chunks - 2, -1, -1):
            out = jnp.where(in_chunk[ci], full[ci * c:(ci + 1) * c], out)
        return out

    def block_diag_of(packed):
        return jnp.concatenate([packed.astype(BF16)] * n_chunks, axis=0) * block_diag

    def mm_packed(packed, bd):
        return jnp.dot(packed.astype(BF16), bd, preferred_element_type=F32)

    scale = hd ** -0.5
    nw = nw_ref[...]
    heads = range(n_heads)

    def each(fn, *per_head):
        return [fn(*args) for args in zip(*per_head)]

    kb = [qkv_ref[:, width + h * hd:width + (h + 1) * hd] for h in heads]
    q = [qkv_ref[:, h * hd:(h + 1) * hd].astype(F32) * scale for h in heads]
    k = each(lambda x: x.astype(F32), kb)
    v = [qkv_ref[:, 2 * width + h * hd:2 * width + (h + 1) * hd].astype(F32) for h in heads]
    beta = [beta_all[:, h:h + 1] for h in heads]
    cg = [cg_all[:, n_heads + h:n_heads + h + 1] for h in heads]
    cgl = [cgl_all[:, n_heads + h:n_heads + h + 1] for h in heads]
    cg_row = [cgt_all[n_heads + h:n_heads + h + 1, :] for h in heads]

    decay = each(lambda a, b: jnp.exp(jnp.where(causal_p, pack(a) - b, -jnp.inf)), cg, cg_row)
    kk = each(lambda x: pack(_mm_nt(x, x)), kb)
    qk = each(lambda x, y, d: block_diag_of(pack(_mm_nt(x, y)) * d), q, kb, decay)
    a_mat = each(lambda b, x, d: jnp.where(strict_p, pack(b) * x * d, 0.0), beta, kk, decay)

    n1 = each(lambda a: -jnp.where(diag16, a, 0.0), a_mat)
    inv = each(lambda n: eye_p + n, n1)
    power = n1
    power_bd = each(block_diag_of, power)
    for _ in range(3):
        power = each(mm_packed, power, power_bd)
        power_bd = each(block_diag_of, power)
        inv = each(lambda t, p: t + mm_packed(t, p), inv, power_bd)
    for off in (off16, off32):
        left = each(lambda t, a: mm_packed(t, block_diag_of(jnp.where(off, a, 0.0))), inv, a_mat)
        inv = each(lambda t, x: t - mm_packed(x, block_diag_of(t)), inv, left)

    ecg = each(jnp.exp, cg)
    rhs = each(lambda vv, kx, b, e: jnp.concatenate([vv * b, kx * (b * e)], axis=1),
               v, k, beta, ecg)
    uw = each(lambda t, x: jnp.dot(block_diag_of(t), x.astype(BF16), preferred_element_type=F32),
              inv, rhs)
    q_dec = each(lambda x, e: x * e, q, ecg)
    k_dec = each(lambda kx, a, b: (kx * jnp.exp(a - b)).astype(BF16), k, cgl, cg)
    g_last = each(jnp.exp, cgl)

    state = [state_ref[h] for h in heads]
    v_new = [[] for _ in heads]
    o_state = [[] for _ in heads]
    for ci in range(n_chunks):
        rs = slice(ci * c, (ci + 1) * c)
        r = each(lambda x, qd, s: _mm(jnp.concatenate([x[rs, hd:], qd[rs]], axis=0), s),
                 uw, q_dec, state)
        vn = each(lambda x, y: x[rs, :hd] - y[:c], uw, r)
        state = each(lambda s, g, kd, x: s * g[ci * c:ci * c + 1, :] + _mm_tn(kd[rs], x),
                     state, g_last, k_dec, vn)
        for h in heads:
            v_new[h].append(vn[h])
            o_state[h].append(r[h][c:])
    for h in heads:
        state_ref[h] = state[h]
    o = each(lambda os, x, vs: jnp.concatenate(os, axis=0) + _mm(x, jnp.concatenate(vs, axis=0)),
             o_state, qk, v_new)

    for h in heads:
        on = o[h] * lax.rsqrt(jnp.mean(o[h] * o[h], axis=-1, keepdims=True) + EPS) * nw
        zz = z_ref[:, h * hd:(h + 1) * hd].astype(F32)
        o_ref[:, h * hd:(h + 1) * hd] = (on * _silu(zz)).astype(o_ref.dtype)


def _gdn(proj, ba, alog_row, dtb_row, norm_w, b_sz, t_len, n_heads, z_block):
    width = n_heads * HEAD_DIM
    tt = min(GDN_TILE, t_len)
    n_t = t_len // tt
    assert t_len % tt == 0 and tt % GDN_CHUNK == 0
    return pl.pallas_call(
        functools.partial(_gdn_kernel, n_heads=n_heads),
        grid=(b_sz, n_t),
        in_specs=[
            pl.BlockSpec((tt, 3 * width), lambda b, t: (b * n_t + t, 0)),
            pl.BlockSpec((tt, width), lambda b, t: (b * n_t + t, z_block)),
            pl.BlockSpec((tt, HEAD_DIM), lambda b, t: (b * n_t + t, 0)),
            pl.BlockSpec((1, HEAD_DIM), lambda b, t: (0, 0)),
            pl.BlockSpec((1, HEAD_DIM), lambda b, t: (0, 0)),
            pl.BlockSpec((1, HEAD_DIM), lambda b, t: (0, 0)),
        ],
        out_specs=pl.BlockSpec((tt, width), lambda b, t: (b * n_t + t, 0)),
        out_shape=jax.ShapeDtypeStruct((b_sz * t_len, width), BF16),
        scratch_shapes=[pltpu.VMEM((n_heads, HEAD_DIM, HEAD_DIM), F32)],
        compiler_params=pltpu.CompilerParams(
            dimension_semantics=("parallel", "arbitrary"), vmem_limit_bytes=VMEM_LIMIT),
        name="gdn",
    )(proj, proj, ba, alog_row, dtb_row, norm_w)


def _moba_kernel(pair_q_ref, pair_k_ref, q_ref, k_ref, v_ref, z_ref, o_ref,
                 qaug_s, kaug_s, vt_s, kmean_s, m_s, l_s, acc_s, *, unroll):
    t_len, hd = q_ref.shape
    bs = MOBA_BLOCK
    n_blk = t_len // bs
    n_pairs = pair_q_ref.shape[0]
    gate_rows = -(-n_blk // CARRY_ROWS) * CARRY_ROWS
    exp_scale = hd ** -0.5 * math.log2(math.e)

    def rows(i):
        return pl.ds(pl.multiple_of(i * bs, bs), bs)

    def softmax_weights(s, m):
        return jnp.exp2(((s - m) * exp_scale).astype(BF16))


    def prep_load(i):
        return (i, q_ref[rows(i), :], k_ref[rows(i), :], v_ref[rows(i), :])

    def prep_compute(i, q, k, v):
        lane = lax.broadcasted_iota(jnp.int32, (bs, hd), 1)
        k_aug = jnp.concatenate([k, (lane == i).astype(BF16)], axis=1)
        v_t = jnp.concatenate([v.astype(F32).T, jnp.ones((SUM_ROWS, bs), F32)], axis=0)
        return q, k_aug, jnp.mean(k.astype(F32), axis=0, keepdims=True), v_t.astype(BF16)

    def prep_store(i, q, k_aug, k_mean, v_t):
        qaug_s[rows(i), :hd] = q
        kaug_s[rows(i), :] = k_aug
        kmean_s[pl.ds(i, 1), :] = k_mean
        vt_s[i] = v_t

    def own_load(i):
        return (i, qaug_s[rows(i), :hd], kaug_s[rows(i), :hd], vt_s[i], kmean_s[...])

    def own_compute(i, qi, ki, vi_t, k_mean):
        blk = lax.broadcasted_iota(jnp.int32, (gate_rows, bs), 0)
        gate = jnp.where(blk < i, _mm_nt(k_mean, qi)[:gate_rows], -jnp.inf)
        sel = blk < 0
        for _ in range(MOBA_TOPK):
            best = jnp.max(gate, axis=0, keepdims=True)
            is_best = (gate == best) & (gate > -jnp.inf)
            pick = blk == jnp.min(jnp.where(is_best, blk, hd), axis=0, keepdims=True)
            sel = sel | pick
            gate = jnp.where(pick, -jnp.inf, gate)
        bias_t = jnp.concatenate([jnp.where(sel, 0.0, MASK_BIAS),
                                  jnp.full((hd - gate_rows, bs), MASK_BIAS, F32)], axis=0)

        key = lax.broadcasted_iota(jnp.int32, (bs, bs), 0)
        qry = lax.broadcasted_iota(jnp.int32, (bs, bs), 1)
        s = jnp.where(key <= qry, _mm_nt(ki, qi), -jnp.inf)
        m0 = jnp.max(s, axis=0, keepdims=True)
        p = softmax_weights(s, m0)
        l0 = jnp.sum(p.astype(F32), axis=0, keepdims=True)
        return bias_t.T.astype(BF16), m0, l0, jnp.dot(vi_t[:hd], p, preferred_element_type=F32)

    def own_store(i, bias, m0, l0, acc0):
        qaug_s[rows(i), hd:] = bias
        m_s[i] = m0
        l_s[i] = l0
        acc_s[i] = acc0

    n_groups = n_pairs // unroll

    def group(g):
        g = jnp.minimum(g, n_groups - 1)
        return [(pair_q_ref[g * unroll + u], pair_k_ref[g * unroll + u]) for u in range(unroll)]

    def weights(pairs):
        loaded = [(kaug_s[rows(j), :], qaug_s[rows(i), :], m_s[i]) for i, j in pairs]
        out = []
        for k_aug, q_aug, m_old in loaded:
            s = _mm_nt(k_aug, q_aug)
            m_new = jnp.maximum(m_old, jnp.max(s, axis=0, keepdims=True))
            out.append((m_new, jnp.exp2((m_old - m_new) * exp_scale), softmax_weights(s, m_new)))
        for (i, _), w in zip(pairs, out):
            m_s[i] = w[0]
        return tuple(w[1:] for w in out)

    def past_body(t, w_now):
        now = group(t)
        state = [(vt_s[j], l_s[i], acc_s[i]) for i, j in now]
        folded = []
        for (v_t, l_prev, acc_prev), (alpha, p) in zip(state, w_now):
            pv = jnp.dot(v_t, p, preferred_element_type=F32)
            folded.append((alpha * l_prev + pv[hd:hd + 1], alpha * acc_prev + pv[:hd]))
        w_ahead = weights(group(t + 1))
        for (i, _), (l_new, acc_new) in zip(now, folded):
            l_s[i] = l_new
            acc_s[i] = acc_new
        return w_ahead

    def finish_load(i):
        return (i, acc_s[i], l_s[i], z_ref[rows(i), :])

    def finish_compute(i, acc, l_fin, z):
        zz = z.astype(F32)
        return (((acc / l_fin).T * _silu(zz)).astype(o_ref.dtype),)

    def finish_store(i, out):
        o_ref[rows(i), :] = out

    def run(load, compute, store, count):
        def body(t, _):
            loaded = [load(t * unroll + u) for u in range(unroll)]
            results = [compute(*vals) for vals in loaded]
            for vals, res in zip(loaded, results):
                store(vals[0], *res)
            return 0
        lax.fori_loop(0, count // unroll, body, 0)

    kmean_s[...] = jnp.zeros_like(kmean_s)
    run(prep_load, prep_compute, prep_store, n_blk)
    run(own_load, own_compute, own_store, n_blk)
    if n_pairs:
        lax.fori_loop(0, n_groups, past_body, weights(group(0)))
    run(finish_load, finish_compute, finish_store, n_blk)


def _moba_pair_schedule(n_blk):
    for unroll in (8, 4, 2, 1):
        if n_blk % unroll:
            continue
        todo = {i: list(range(i)) for i in range(1, n_blk)}
        pairs = []
        while any(todo.values()):
            busiest = sorted((i for i in todo if todo[i]), key=lambda i: -len(todo[i]))[:unroll]
            if len(busiest) < unroll:
                break
            pairs += [(i, todo[i].pop()) for i in busiest]
        else:
            return unroll, pairs
    raise AssertionError("unroll == 1 always schedules")


def _moba(proj, b_sz, t_len, n_heads, q_block0, z_block0):
    width = n_heads * HEAD_DIM
    n_blk = t_len // MOBA_BLOCK
    unroll, pairs = _moba_pair_schedule(n_blk)
    pair_q = jnp.asarray([p[0] for p in pairs], jnp.int32)
    pair_k = jnp.asarray([p[1] for p in pairs], jnp.int32)

    def col(base):
        return lambda b, h, pq, pk: (b, base + h)

    return pl.pallas_call(
        functools.partial(_moba_kernel, unroll=unroll),
        grid_spec=pltpu.PrefetchScalarGridSpec(
            num_scalar_prefetch=2,
            grid=(b_sz, n_heads),
            in_specs=[
                pl.BlockSpec((t_len, HEAD_DIM), col(q_block0)),
                pl.BlockSpec((t_len, HEAD_DIM), col(q_block0 + n_heads)),
                pl.BlockSpec((t_len, HEAD_DIM), col(q_block0 + 2 * n_heads)),
                pl.BlockSpec((t_len, HEAD_DIM), col(z_block0)),
            ],
            out_specs=pl.BlockSpec((t_len, HEAD_DIM), col(0)),
            scratch_shapes=[
                pltpu.VMEM((t_len, 2 * HEAD_DIM), BF16),
                pltpu.VMEM((t_len, 2 * HEAD_DIM), BF16),
                pltpu.VMEM((n_blk, HEAD_DIM + SUM_ROWS, MOBA_BLOCK), BF16),
                pltpu.VMEM((HEAD_DIM, HEAD_DIM), F32),
                pltpu.VMEM((n_blk, 1, MOBA_BLOCK), F32),
                pltpu.VMEM((n_blk, 1, MOBA_BLOCK), F32),
                pltpu.VMEM((n_blk, HEAD_DIM, MOBA_BLOCK), F32),
            ],
        ),
        out_shape=jax.ShapeDtypeStruct((b_sz * t_len, width), BF16),
        compiler_params=pltpu.CompilerParams(
            dimension_semantics=("parallel", "parallel"), vmem_limit_bytes=VMEM_LIMIT),
        name="moba",
    )(pair_q, pair_k, proj, proj, proj, proj)


def _merge_kernel(x_ref, oa_ref, ob_ref, ga_ref, gb_ref, wa_ref, wb_ref, wo_ref, o_ref):
    ya = jnp.dot(oa_ref[...], wa_ref[...], preferred_element_type=F32)
    yb = jnp.dot(ob_ref[...], wb_ref[...], preferred_element_type=F32)
    merged = (_sigmoid(ga_ref[...].astype(F32)) * ya + _sigmoid(gb_ref[...].astype(F32)) * yb)
    o_ref[...] = x_ref[...] + jnp.dot(merged.astype(BF16), wo_ref[...],
                                      preferred_element_type=F32)


def _merge(x2, oa, ob, proj, wa, wb, wo, tm, gate_block0):
    m, d = x2.shape
    wa_rows, wb_rows = wa.shape[0], wb.shape[0]
    resident = dict(pipeline_mode=pl.Buffered(1))
    return pl.pallas_call(
        _merge_kernel,
        grid=(m // tm,),
        in_specs=[
            pl.BlockSpec((tm, d), lambda i: (i, 0)),
            pl.BlockSpec((tm, wa_rows), lambda i: (i, 0)),
            pl.BlockSpec((tm, wb_rows), lambda i: (i, 0)),
            pl.BlockSpec((tm, d), lambda i: (i, gate_block0)),
            pl.BlockSpec((tm, d), lambda i: (i, gate_block0 + 1)),
            pl.BlockSpec((wa_rows, d), lambda i: (0, 0), **resident),
            pl.BlockSpec((wb_rows, d), lambda i: (0, 0), **resident),
            pl.BlockSpec((d, d), lambda i: (0, 0), **resident),
        ],
        out_specs=pl.BlockSpec((tm, d), lambda i: (i, 0)),
        out_shape=jax.ShapeDtypeStruct((m, d), F32),
        compiler_params=pltpu.CompilerParams(
            dimension_semantics=("parallel",), vmem_limit_bytes=VMEM_LIMIT),
        name="merge_out",
    )(x2, oa, ob, proj, proj, wa, wb, wo)


def _lane_row(vec, offset):
    n = vec.shape[0]
    return jnp.pad(vec.astype(F32), (offset, HEAD_DIM - offset - n)).reshape(1, HEAD_DIM)


def _layer(x2, b_sz, t_len, cos_t, sin_t, norm_w, w_in, conv_w, a_log, dt_bias, gdn_norm_w,
           q_norm_w, k_norm_w, w_out_gdn, w_out_moba, w_o):
    m, d = x2.shape
    gw = w_out_gdn.shape[0]
    mw = w_out_moba.shape[0]
    gh = a_log.shape[0]
    mh = mw // HEAD_DIM
    assert gw == gh * HEAD_DIM and 2 * gh <= HEAD_DIM
    assert gw == mw and (8 * gw) % d == 0 and (2 * d) % gw == 0

    c1 = 4 * gw
    c2 = c1 + 2 * gh
    w_main = jnp.concatenate([w_in[:, :c1], w_in[:, c2:]], axis=1).astype(BF16)
    w_ba = jnp.pad(w_in[:, c1:c2], ((0, 0), (0, HEAD_DIM - 2 * gh))).astype(BF16)
    gdn_z_block = 3
    moba_q_block = _TILE_MOBA_Q * gw // HEAD_DIM
    moba_z_block = 7 * gw // HEAD_DIM
    gate_block = 8 * gw // d

    tm = math.gcd(1024, t_len)
    qk_norm_w = jnp.stack([q_norm_w, k_norm_w]).astype(F32)
    proj, ba = _in_proj(x2, norm_w.reshape(1, d), w_main, w_ba, conv_w, cos_t, sin_t, qk_norm_w,
                        tm, gw, t_len)

    oa = _gdn(proj, ba, _lane_row(a_log, gh), _lane_row(dt_bias, gh),
              gdn_norm_w.reshape(1, HEAD_DIM), b_sz, t_len, gh, gdn_z_block)
    ob = _moba(proj, b_sz, t_len, mh, moba_q_block, moba_z_block)
    return _merge(x2, oa, ob, proj, w_out_gdn.astype(BF16), w_out_moba.astype(BF16),
                  w_o.astype(BF16), min(256, m), gate_block)


def kernel(x, norm_w, w_in, gdn_conv_w, gdn_a_log, gdn_dt_bias, gdn_norm_w, moba_q_norm_w,
           moba_k_norm_w, w_out_gdn, w_out_moba, w_o):
    b_sz, t_len, d = x.shape
    cos_t, sin_t = _rope_tables(t_len)
    x2 = x.reshape(b_sz * t_len, d)
    for l in range(norm_w.shape[0]):
        x2 = _layer(x2, b_sz, t_len, cos_t, sin_t, norm_w[l], w_in[l], gdn_conv_w[l],
                    gdn_a_log[l], gdn_dt_bias[l], gdn_norm_w[l], moba_q_norm_w[l],
                    moba_k_norm_w[l], w_out_gdn[l], w_out_moba[l], w_o[l])
    return x2.reshape(b_sz, t_len, d)
```

```python
import functools
import math

import jax
import jax.numpy as jnp
from jax import lax
from jax.experimental import pallas as pl
from jax.experimental.pallas import tpu as pltpu

F32 = jnp.float32
BF16 = jnp.bfloat16

EPS = 1e-6
ROPE_THETA = 10000.0
HEAD_DIM = 128
GDN_CONV = 4
GDN_CHUNK = 64
GDN_TILE = 256
MOBA_BLOCK = 256
MOBA_TOPK = 3
MASK_BIAS = -1e30
CARRY_ROWS = 8
SUM_ROWS = 16
VMEM_LIMIT = 56 * 1024 * 1024


def _mm(a, b):
    return jnp.dot(a.astype(BF16), b.astype(BF16), preferred_element_type=F32)


def _mm_nt(a, b):
    return lax.dot_general(a.astype(BF16), b.astype(BF16), (((1,), (1,)), ((), ())),
                           preferred_element_type=F32)


def _mm_tn(a, b):
    return lax.dot_general(a.astype(BF16), b.astype(BF16), (((0,), (0,)), ((), ())),
                           preferred_element_type=F32)


def _mm_exact(a, b):
    return jnp.dot(a, b, preferred_element_type=F32, precision=lax.Precision.HIGHEST)


def _sigmoid(x):
    return 0.5 * jnp.tanh(0.5 * x) + 0.5


def _silu(x):
    h = 0.5 * x
    return h + h * jnp.tanh(h)


def _rope_table_kernel(cos_ref, sin_ref):
    t_len, d = cos_ref.shape
    lane = lax.broadcasted_iota(jnp.int32, (t_len, d), 1)
    pos = lax.broadcasted_iota(jnp.int32, (t_len, d), 0).astype(F32)
    half = d // 2
    pair = jnp.where(lane < half, lane, lane - half).astype(F32)
    inv_freq = jnp.exp(pair * (-2.0 * math.log(ROPE_THETA) / d))
    ang = pos * inv_freq
    cos_ref[...] = jnp.cos(ang)
    sin_ref[...] = jnp.where(lane < half, -jnp.sin(ang), jnp.sin(ang))


def _rope_tables(t_len):
    return pl.pallas_call(
        _rope_table_kernel,
        out_shape=(jax.ShapeDtypeStruct((t_len, HEAD_DIM), F32),
                   jax.ShapeDtypeStruct((t_len, HEAD_DIM), F32)),
        name="rope_tables",
    )()


_TILE_GDN_V = 2
_TILE_GDN_Z = 3
_TILE_MOBA_Q = 4
_TILE_MOBA_K = 5
EPILOGUE_ROWS = 256


def _in_proj_kernel(x_ref, nw_ref, wa_ref, wb_ref, wba_ref, convw_ref, cos_ref, sin_ref, qkw_ref,
                    o_ref, ba_ref, h_ref, carry_ref, *, rows_per_seq):
    i = pl.program_id(0)
    j = pl.program_id(1)
    tm, tn = o_ref.shape
    hd = HEAD_DIM
    slab = min(EPILOGUE_ROWS, tm)

    @pl.when((i == 0) & (j == 0))
    def _():
        carry_ref[...] = jnp.zeros_like(carry_ref)

    @pl.when(j == 0)
    def _():
        x = x_ref[...]
        var = jnp.mean(x * x, axis=-1, keepdims=True)
        h = (x * lax.rsqrt(var + EPS) * nw_ref[...]).astype(BF16)
        h_ref[...] = h
        ba_ref[...] = jnp.dot(h, wba_ref[...], preferred_element_type=F32)

    if tm >= 4 * slab:
        edges = [0, slab // 2, *range(slab // 2 + slab, tm - slab // 2, slab), tm - slab // 2, tm]
    else:
        edges = list(range(0, tm + 1, slab))
    slabs = list(zip(edges[:-1], edges[1:]))

    def project(r0, r1, w_ref):
        return jnp.dot(h_ref[r0:r1, :], w_ref[...], preferred_element_type=F32)

    def plain(w_ref):
        for r0 in range(0, tm, slab):
            o_ref[r0:r0 + slab, :] = project(r0, r0 + slab, w_ref).astype(o_ref.dtype)

    @pl.when(j <= _TILE_GDN_V)
    def _():
        seq_start = (i * tm) % rows_per_seq == 0
        prev = jnp.where(seq_start, 0.0, carry_ref[j])
        wv = convw_ref[...]
        for r0, r1 in slabs:
            acc = project(r0, r1, wa_ref)
            window = jnp.concatenate([prev, acc], axis=0)
            y = acc * wv[GDN_CONV - 1:GDN_CONV, :]
            for s in range(1, GDN_CONV):
                y = y + (pltpu.roll(window, s, axis=0)[CARRY_ROWS:]
                         * wv[GDN_CONV - 1 - s:GDN_CONV - s, :])
            prev = acc[r1 - r0 - CARRY_ROWS:]
            y = _silu(y)
            for h in range(tn // hd):
                yh = y[:, h * hd:(h + 1) * hd]
                l2 = lax.rsqrt(jnp.sum(yh * yh, axis=-1, keepdims=True) + EPS)
                yh = yh * jnp.where(j < _TILE_GDN_V, l2, 1.0)
                o_ref[r0:r1, h * hd:(h + 1) * hd] = yh.astype(o_ref.dtype)
        carry_ref[j] = prev

    @pl.when(j == _TILE_GDN_Z)
    def _():
        plain(wa_ref)

    @pl.when((j == _TILE_MOBA_Q) | (j == _TILE_MOBA_K))
    def _():
        w = jnp.where(j == _TILE_MOBA_Q, qkw_ref[0:1, :], qkw_ref[1:2, :])
        for r0, r1 in slabs:
            acc = project(r0, r1, wb_ref)
            cos = cos_ref[r0:r1, :]
            sin = sin_ref[r0:r1, :]
            for h in range(tn // hd):
                x = acc[:, h * hd:(h + 1) * hd]
                y = x * lax.rsqrt(jnp.mean(x * x, axis=-1, keepdims=True) + EPS) * w
                y = y * cos + pltpu.roll(y, hd // 2, axis=1) * sin
                o_ref[r0:r1, h * hd:(h + 1) * hd] = y.astype(o_ref.dtype)

    @pl.when(j > _TILE_MOBA_K)
    def _():
        plain(wb_ref)


def _in_proj(x2, norm_w, w_a, w_b, w_ba, conv_w, cos_t, sin_t, qk_norm_w, tm, tn, t_len):
    m, d = x2.shape
    a_tiles = _TILE_GDN_Z + 1
    n = w_a.shape[1] + w_b.shape[1]
    assert t_len % tm == 0 and conv_w.shape[1] == (_TILE_GDN_V + 1) * tn
    assert w_a.shape[1] == a_tiles * tn and w_b.shape[1] % tn == 0
    seq_tiles = t_len // tm
    return pl.pallas_call(
        functools.partial(_in_proj_kernel, rows_per_seq=t_len),
        grid=(m // tm, n // tn),
        in_specs=[
            pl.BlockSpec((tm, d), lambda i, j: (i, 0)),
            pl.BlockSpec((1, d), lambda i, j: (0, 0)),
            pl.BlockSpec((d, tn), lambda i, j: (0, jnp.minimum(j, a_tiles - 1))),
            pl.BlockSpec((d, tn), lambda i, j: (0, jnp.maximum(j - a_tiles, 0))),
            pl.BlockSpec((d, HEAD_DIM), lambda i, j: (0, 0)),
            pl.BlockSpec((GDN_CONV, tn), lambda i, j: (0, jnp.minimum(j, _TILE_GDN_V))),
            pl.BlockSpec((tm, HEAD_DIM), lambda i, j: (i % seq_tiles, 0)),
            pl.BlockSpec((tm, HEAD_DIM), lambda i, j: (i % seq_tiles, 0)),
            pl.BlockSpec((2, HEAD_DIM), lambda i, j: (0, 0)),
        ],
        out_specs=(
            pl.BlockSpec((tm, tn), lambda i, j: (i, j)),
            pl.BlockSpec((tm, HEAD_DIM), lambda i, j: (i, 0)),
        ),
        out_shape=(jax.ShapeDtypeStruct((m, n), BF16),
                   jax.ShapeDtypeStruct((m, HEAD_DIM), F32)),
        scratch_shapes=[
            pltpu.VMEM((tm, d), BF16),
            pltpu.VMEM((_TILE_GDN_V + 1, CARRY_ROWS, tn), F32),
        ],
        compiler_params=pltpu.CompilerParams(
            dimension_semantics=("arbitrary", "arbitrary"), vmem_limit_bytes=VMEM_LIMIT),
        name="in_proj",
    )(x2, norm_w, w_a, w_b, w_ba, conv_w, cos_t, sin_t, qk_norm_w)


def _gdn_kernel(qkv_ref, z_ref, ba_ref, alog_ref, dtb_ref, nw_ref, o_ref, state_ref, *, n_heads):
    tt = qkv_ref.shape[0]
    c = GDN_CHUNK
    n_chunks = tt // c
    hd = HEAD_DIM
    width = n_heads * hd

    @pl.when(pl.program_id(1) == 0)
    def _():
        state_ref[...] = jnp.zeros_like(state_ref)

    ba = ba_ref[...]
    beta_all = _sigmoid(ba)
    sp_in = ba + dtb_ref[...]
    softplus = jnp.maximum(sp_in, 0.0) + jnp.log1p(jnp.exp(-jnp.abs(sp_in)))
    g_all = -jnp.exp(alog_ref[...]) * softplus

    row = lax.broadcasted_iota(jnp.int32, (tt, tt), 0)
    col = lax.broadcasted_iota(jnp.int32, (tt, tt), 1)
    same_chunk = (row // c) == (col // c)
    block_diag = same_chunk.astype(BF16)

    cg_all = _mm_exact((same_chunk & (col <= row)).astype(F32), g_all)
    cgl_all = _mm_exact(same_chunk.astype(F32), g_all)
    cgt_all = cg_all.T

    prow = lax.broadcasted_iota(jnp.int32, (c, tt), 0)
    plane = lax.broadcasted_iota(jnp.int32, (c, tt), 1)
    pcol = plane % c
    in_chunk = [plane // c == ci for ci in range(n_chunks)]
    causal_p = pcol <= prow
    strict_p = pcol < prow
    same16 = (prow // 16) == (pcol // 16)
    same32 = (prow // 32) == (pcol // 32)
    diag16 = strict_p & same16
    off16 = strict_p & same32 & jnp.logical_not(same16)
    off32 = strict_p & jnp.logical_not(same32)
    eye_p = (pcol == prow).astype(F32)

    def pack(full):
        out = full[(n_chunks - 1) * c:]
        for ci in range(n_chunks - 2, -1, -1):
            out = jnp.where(in_chunk[ci], full[ci * c:(ci + 1) * c], out)
        return out

    def block_diag_of(packed):
        return jnp.concatenate([packed.astype(BF16)] * n_chunks, axis=0) * block_diag

    def mm_packed(packed, bd):
        return jnp.dot(packed.astype(BF16), bd, preferred_element_type=F32)

    scale = hd ** -0.5
    nw = nw_ref[...]
    heads = range(n_heads)

    def each(fn, *per_head):
        return [fn(*args) for args in zip(*per_head)]

    kb = [qkv_ref[:, width + h * hd:width + (h + 1) * hd] for h in heads]
    q = [qkv_ref[:, h * hd:(h + 1) * hd].astype(F32) * scale for h in heads]
    k = each(lambda x: x.astype(F32), kb)
    v = [qkv_ref[:, 2 * width + h * hd:2 * width + (h + 1) * hd].astype(F32) for h in heads]
    beta = [beta_all[:, h:h + 1] for h in heads]
    cg = [cg_all[:, n_heads + h:n_heads + h + 1] for h in heads]
    cgl = [cgl_all[:, n_heads + h:n_heads + h + 1] for h in heads]
    cg_row = [cgt_all[n_heads + h:n_heads + h + 1, :] for h in heads]

    decay = each(lambda a, b: jnp.exp(jnp.where(causal_p, pack(a) - b, -jnp.inf)), cg, cg_row)
    kk = each(lambda x: pack(_mm_nt(x, x)), kb)
    qk = each(lambda x, y, d: block_diag_of(pack(_mm_nt(x, y)) * d), q, kb, decay)
    a_mat = each(lambda b, x, d: jnp.where(strict_p, pack(b) * x * d, 0.0), beta, kk, decay)

    n1 = each(lambda a: -jnp.where(diag16, a, 0.0), a_mat)
    inv = each(lambda n: eye_p + n, n1)
    power = n1
    power_bd = each(block_diag_of, power)
    for _ in range(3):
        power = each(mm_packed, power, power_bd)
        power_bd = each(block_diag_of, power)
        inv = each(lambda t, p: t + mm_packed(t, p), inv, power_bd)
    for off in (off16, off32):
        left = each(lambda t, a: mm_packed(t, block_diag_of(jnp.where(off, a, 0.0))), inv, a_mat)
        inv = each(lambda t, x: t - mm_packed(x, block_diag_of(t)), inv, left)

    ecg = each(jnp.exp, cg)
    rhs = each(lambda vv, kx, b, e: jnp.concatenate([vv * b, kx * (b * e)], axis=1),
               v, k, beta, ecg)
    uw = each(lambda t, x: jnp.dot(block_diag_of(t), x.astype(BF16), preferred_element_type=F32),
              inv, rhs)
    q_dec = each(lambda x, e: x * e, q, ecg)
    k_dec = each(lambda kx, a, b: (kx * jnp.exp(a - b)).astype(BF16), k, cgl, cg)
    g_last = each(jnp.exp, cgl)

    state = [state_ref[h] for h in heads]
    v_new = [[] for _ in heads]
    o_state = [[] for _ in heads]
    for ci in range(n_chunks):
        rs = slice(ci * c, (ci + 1) * c)
        r = each(lambda x, qd, s: _mm(jnp.concatenate([x[rs, hd:], qd[rs]], axis=0), s),
                 uw, q_dec, state)
        vn = each(lambda x, y: x[rs, :hd] - y[:c], uw, r)
        state = each(lambda s, g, kd, x: s * g[ci * c:ci * c + 1, :] + _mm_tn(kd[rs], x),
                     state, g_last, k_dec, vn)
        for h in heads:
            v_new[h].append(vn[h])
            o_state[h].append(r[h][c:])
    for h in heads:
        state_ref[h] = state[h]
    o = each(lambda os, x, vs: jnp.concatenate(os, axis=0) + _mm(x, jnp.concatenate(vs, axis=0)),
             o_state, qk, v_new)

    for h in heads:
        on = o[h] * lax.rsqrt(jnp.mean(o[h] * o[h], axis=-1, keepdims=True) + EPS) * nw
        zz = z_ref[:, h * hd:(h + 1) * hd].astype(F32)
        o_ref[:, h * hd:(h + 1) * hd] = (on * _silu(zz)).astype(o_ref.dtype)


def _gdn(proj, ba, alog_row, dtb_row, norm_w, b_sz, t_len, n_heads, z_block):
    width = n_heads * HEAD_DIM
    tt = min(GDN_TILE, t_len)
    n_t = t_len // tt
    assert t_len % tt == 0 and tt % GDN_CHUNK == 0
    return pl.pallas_call(
        functools.partial(_gdn_kernel, n_heads=n_heads),
        grid=(b_sz, n_t),
        in_specs=[
            pl.BlockSpec((tt, 3 * width), lambda b, t: (b * n_t + t, 0)),
            pl.BlockSpec((tt, width), lambda b, t: (b * n_t + t, z_block)),
            pl.BlockSpec((tt, HEAD_DIM), lambda b, t: (b * n_t + t, 0)),
            pl.BlockSpec((1, HEAD_DIM), lambda b, t: (0, 0)),
            pl.BlockSpec((1, HEAD_DIM), lambda b, t: (0, 0)),
            pl.BlockSpec((1, HEAD_DIM), lambda b, t: (0, 0)),
        ],
        out_specs=pl.BlockSpec((tt, width), lambda b, t: (b * n_t + t, 0)),
        out_shape=jax.ShapeDtypeStruct((b_sz * t_len, width), BF16),
        scratch_shapes=[pltpu.VMEM((n_heads, HEAD_DIM, HEAD_DIM), F32)],
        compiler_params=pltpu.CompilerParams(
            dimension_semantics=("parallel", "arbitrary"), vmem_limit_bytes=VMEM_LIMIT),
        name="gdn",
    )(proj, proj, ba, alog_row, dtb_row, norm_w)


def _moba_kernel(q_ref, k_ref, v_ref, z_ref, o_ref,
                 qaug_s, kaug_s, vt_s, kmean_s, m_s, l_s, acc_s, *, unroll, pairs):
    t_len, hd = q_ref.shape
    bs = MOBA_BLOCK
    n_blk = t_len // bs
    gate_rows = -(-n_blk // CARRY_ROWS) * CARRY_ROWS
    exp_scale = hd ** -0.5 * math.log2(math.e)

    def rows(i):
        return pl.ds(pl.multiple_of(i * bs, bs), bs)

    def softmax_weights(s, m):
        return jnp.exp2(((s - m) * exp_scale).astype(BF16))


    def prep_load(i):
        return (i, q_ref[rows(i), :], k_ref[rows(i), :], v_ref[rows(i), :])

    def prep_compute(i, q, k, v):
        lane = lax.broadcasted_iota(jnp.int32, (bs, hd), 1)
        k_aug = jnp.concatenate([k, (lane == i).astype(BF16)], axis=1)
        v_t = jnp.concatenate([v.astype(F32).T, jnp.ones((SUM_ROWS, bs), F32)], axis=0)
        return q, k_aug, jnp.mean(k.astype(F32), axis=0, keepdims=True), v_t.astype(BF16)

    def prep_store(i, q, k_aug, k_mean, v_t):
        qaug_s[rows(i), :hd] = q
        kaug_s[rows(i), :] = k_aug
        kmean_s[pl.ds(i, 1), :] = k_mean
        vt_s[i] = v_t

    def own_load(i):
        return (i, qaug_s[rows(i), :hd], kaug_s[rows(i), :hd], vt_s[i], kmean_s[...])

    def own_compute(i, qi, ki, vi_t, k_mean):
        blk = lax.broadcasted_iota(jnp.int32, (gate_rows, bs), 0)
        gate = jnp.where(blk < i, _mm_nt(k_mean, qi)[:gate_rows], -jnp.inf)
        sel = blk < 0
        for _ in range(MOBA_TOPK):
            best = jnp.max(gate, axis=0, keepdims=True)
            is_best = (gate == best) & (gate > -jnp.inf)
            pick = blk == jnp.min(jnp.where(is_best, blk, hd), axis=0, keepdims=True)
            sel = sel | pick
            gate = jnp.where(pick, -jnp.inf, gate)
        bias_t = jnp.concatenate([jnp.where(sel, 0.0, MASK_BIAS),
                                  jnp.full((hd - gate_rows, bs), MASK_BIAS, F32)], axis=0)

        key = lax.broadcasted_iota(jnp.int32, (bs, bs), 0)
        qry = lax.broadcasted_iota(jnp.int32, (bs, bs), 1)
        s = jnp.where(key <= qry, _mm_nt(ki, qi), -jnp.inf)
        m0 = jnp.max(s, axis=0, keepdims=True)
        p = softmax_weights(s, m0)
        l0 = jnp.sum(p.astype(F32), axis=0, keepdims=True)
        return bias_t.T.astype(BF16), m0, l0, jnp.dot(vi_t[:hd], p, preferred_element_type=F32)

    def own_store(i, bias, m0, l0, acc0):
        qaug_s[rows(i), hd:] = bias
        m_s[i] = m0
        l_s[i] = l0
        acc_s[i] = acc0

    def past_group(group):
        weights = []
        for i, j in group:
            s = _mm_nt(kaug_s[j * bs:(j + 1) * bs, :], qaug_s[i * bs:(i + 1) * bs, :])
            m_old = m_s[i]
            m_new = jnp.maximum(m_old, jnp.max(s, axis=0, keepdims=True))
            m_s[i] = m_new
            weights.append((jnp.exp2((m_old - m_new) * exp_scale), softmax_weights(s, m_new)))
        for (i, j), (alpha, p) in zip(group, weights):
            pv = jnp.dot(vt_s[j], p, preferred_element_type=F32)
            l_s[i] = alpha * l_s[i] + pv[hd:hd + 1]
            acc_s[i] = alpha * acc_s[i] + pv[:hd]

    def finish_load(i):
        return (i, acc_s[i], l_s[i], z_ref[rows(i), :])

    def finish_compute(i, acc, l_fin, z):
        zz = z.astype(F32)
        return (((acc / l_fin).T * _silu(zz)).astype(o_ref.dtype),)

    def finish_store(i, out):
        o_ref[rows(i), :] = out

    def run(load, compute, store, count):
        def body(t, _):
            loaded = [load(t * unroll + u) for u in range(unroll)]
            results = [compute(*vals) for vals in loaded]
            for vals, res in zip(loaded, results):
                store(vals[0], *res)
            return 0
        lax.fori_loop(0, count // unroll, body, 0)

    kmean_s[...] = jnp.zeros_like(kmean_s)
    run(prep_load, prep_compute, prep_store, n_blk)
    run(own_load, own_compute, own_store, n_blk)
    for g in range(0, len(pairs), unroll):
        past_group(pairs[g:g + unroll])
    run(finish_load, finish_compute, finish_store, n_blk)


def _moba_pair_schedule(n_blk):
    for unroll in (8, 4, 2, 1):
        if n_blk % unroll:
            continue
        todo = {i: list(range(i)) for i in range(1, n_blk)}
        pairs = []
        while any(todo.values()):
            busiest = sorted((i for i in todo if todo[i]), key=lambda i: -len(todo[i]))[:unroll]
            if len(busiest) < unroll:
                break
            pairs += [(i, todo[i].pop()) for i in busiest]
        else:
            return unroll, pairs
    raise AssertionError("unroll == 1 always schedules")


def _moba(proj, b_sz, t_len, n_heads, q_block0, z_block0):
    width = n_heads * HEAD_DIM
    n_blk = t_len // MOBA_BLOCK
    unroll, pairs = _moba_pair_schedule(n_blk)

    def col(base):
        return lambda b, h: (b, base + h)

    return pl.pallas_call(
        functools.partial(_moba_kernel, unroll=unroll, pairs=tuple(pairs)),
        grid=(b_sz, n_heads),
        in_specs=[
            pl.BlockSpec((t_len, HEAD_DIM), col(q_block0)),
            pl.BlockSpec((t_len, HEAD_DIM), col(q_block0 + n_heads)),
            pl.BlockSpec((t_len, HEAD_DIM), col(q_block0 + 2 * n_heads)),
            pl.BlockSpec((t_len, HEAD_DIM), col(z_block0)),
        ],
        out_specs=pl.BlockSpec((t_len, HEAD_DIM), col(0)),
        scratch_shapes=[
            pltpu.VMEM((t_len, 2 * HEAD_DIM), BF16),
            pltpu.VMEM((t_len, 2 * HEAD_DIM), BF16),
            pltpu.VMEM((n_blk, HEAD_DIM + SUM_ROWS, MOBA_BLOCK), BF16),
            pltpu.VMEM((HEAD_DIM, HEAD_DIM), F32),
            pltpu.VMEM((n_blk, 1, MOBA_BLOCK), F32),
            pltpu.VMEM((n_blk, 1, MOBA_BLOCK), F32),
            pltpu.VMEM((n_blk, HEAD_DIM, MOBA_BLOCK), F32),
        ],
        out_shape=jax.ShapeDtypeStruct((b_sz * t_len, width), BF16),
        compiler_params=pltpu.CompilerParams(
            dimension_semantics=("parallel", "parallel"), vmem_limit_bytes=VMEM_LIMIT),
        name="moba",
    )(proj, proj, proj, proj)


def _merge_kernel(x_ref, oa_ref, ob_ref, ga_ref, gb_ref, wa_ref, wb_ref, wo_ref, o_ref):
    ya = jnp.dot(oa_ref[...], wa_ref[...], preferred_element_type=F32)
    yb = jnp.dot(ob_ref[...], wb_ref[...], preferred_element_type=F32)
    merged = (_sigmoid(ga_ref[...].astype(F32)) * ya + _sigmoid(gb_ref[...].astype(F32)) * yb)
    o_ref[...] = x_ref[...] + jnp.dot(merged.astype(BF16), wo_ref[...],
                                      preferred_element_type=F32)


def _merge(x2, oa, ob, proj, wa, wb, wo, tm, gate_block0):
    m, d = x2.shape
    wa_rows, wb_rows = wa.shape[0], wb.shape[0]
    resident = dict(pipeline_mode=pl.Buffered(1))
    return pl.pallas_call(
        _merge_kernel,
        grid=(m // tm,),
        in_specs=[
            pl.BlockSpec((tm, d), lambda i: (i, 0)),
            pl.BlockSpec((tm, wa_rows), lambda i: (i, 0)),
            pl.BlockSpec((tm, wb_rows), lambda i: (i, 0)),
            pl.BlockSpec((tm, d), lambda i: (i, gate_block0)),
            pl.BlockSpec((tm, d), lambda i: (i, gate_block0 + 1)),
            pl.BlockSpec((wa_rows, d), lambda i: (0, 0), **resident),
            pl.BlockSpec((wb_rows, d), lambda i: (0, 0), **resident),
            pl.BlockSpec((d, d), lambda i: (0, 0), **resident),
        ],
        out_specs=pl.BlockSpec((tm, d), lambda i: (i, 0)),
        out_shape=jax.ShapeDtypeStruct((m, d), F32),
        compiler_params=pltpu.CompilerParams(
            dimension_semantics=("parallel",), vmem_limit_bytes=VMEM_LIMIT),
        name="merge_out",
    )(x2, oa, ob, proj, proj, wa, wb, wo)


def _lane_row(vec, offset):
    n = vec.shape[0]
    return jnp.pad(vec.astype(F32), (offset, HEAD_DIM - offset - n)).reshape(1, HEAD_DIM)


def _layer(x2, b_sz, t_len, cos_t, sin_t, norm_w, w_in, conv_w, a_log, dt_bias, gdn_norm_w,
           q_norm_w, k_norm_w, w_out_gdn, w_out_moba, w_o):
    m, d = x2.shape
    gw = w_out_gdn.shape[0]
    mw = w_out_moba.shape[0]
    gh = a_log.shape[0]
    mh = mw // HEAD_DIM
    assert gw == gh * HEAD_DIM and 2 * gh <= HEAD_DIM
    assert gw == mw and (8 * gw) % d == 0 and (2 * d) % gw == 0

    c1 = 4 * gw
    c2 = c1 + 2 * gh
    w_a = w_in[:, :c1].astype(BF16)
    w_b = w_in[:, c2:].astype(BF16)
    w_ba = jnp.pad(w_in[:, c1:c2], ((0, 0), (0, HEAD_DIM - 2 * gh))).astype(BF16)
    gdn_z_block = 3
    moba_q_block = _TILE_MOBA_Q * gw // HEAD_DIM
    moba_z_block = 7 * gw // HEAD_DIM
    gate_block = 8 * gw // d

    tm = math.gcd(1024, t_len)
    qk_norm_w = jnp.stack([q_norm_w, k_norm_w]).astype(F32)
    proj, ba = _in_proj(x2, norm_w.reshape(1, d), w_a, w_b, w_ba, conv_w, cos_t, sin_t, qk_norm_w,
                        tm, gw, t_len)

    oa = _gdn(proj, ba, _lane_row(a_log, gh), _lane_row(dt_bias, gh),
              gdn_norm_w.reshape(1, HEAD_DIM), b_sz, t_len, gh, gdn_z_block)
    ob = _moba(proj, b_sz, t_len, mh, moba_q_block, moba_z_block)
    return _merge(x2, oa, ob, proj, w_out_gdn.astype(BF16), w_out_moba.astype(BF16),
                  w_o.astype(BF16), min(256, m), gate_block)


def kernel(x, norm_w, w_in, gdn_conv_w, gdn_a_log, gdn_dt_bias, gdn_norm_w, moba_q_norm_w,
           moba_k_norm_w, w_out_gdn, w_out_moba, w_o):
    b_sz, t_len, d = x.shape
    cos_t, sin_t = _rope_tables(t_len)
    x2 = x.reshape(b_sz * t_len, d)
    for l in range(norm_w.shape[0]):
        x2 = _layer(x2, b_sz, t_len, cos_t, sin_t, norm_w[l], w_in[l], gdn_conv_w[l],
                    gdn_a_log[l], gdn_dt_bias[l], gdn_norm_w[l], moba_q_norm_w[l],
                    moba_k_norm_w[l], w_out_gdn[l], w_out_moba[l], w_o[l])
    return x2.reshape(b_sz, t_len, d)
```

```python
import functools
import math

import jax
import jax.numpy as jnp
from jax import lax
from jax.experimental import pallas as pl
from jax.experimental.pallas import tpu as pltpu

F32 = jnp.float32
BF16 = jnp.bfloat16

EPS = 1e-6
ROPE_THETA = 10000.0
HEAD_DIM = 128
GDN_CONV = 4
GDN_CHUNK = 64
GDN_TILE = 256
MOBA_BLOCK = 256
MOBA_TOPK = 3
MASK_BIAS = -1e30
CARRY_ROWS = 8
SUM_ROWS = 16
VMEM_LIMIT = 56 * 1024 * 1024


def _mm(a, b):
    return jnp.dot(a.astype(BF16), b.astype(BF16), preferred_element_type=F32)


def _mm_nt(a, b):
    return lax.dot_general(a.astype(BF16), b.astype(BF16), (((1,), (1,)), ((), ())),
                           preferred_element_type=F32)


def _mm_tn(a, b):
    return lax.dot_general(a.astype(BF16), b.astype(BF16), (((0,), (0,)), ((), ())),
                           preferred_element_type=F32)


def _sigmoid(x):
    return 0.5 * jnp.tanh(0.5 * x) + 0.5


def _silu(x):
    h = 0.5 * x
    return h + h * jnp.tanh(h)


def _rope_table_kernel(cos_ref, sin_ref):
    t_len, d = cos_ref.shape
    lane = lax.broadcasted_iota(jnp.int32, (t_len, d), 1)
    pos = lax.broadcasted_iota(jnp.int32, (t_len, d), 0).astype(F32)
    half = d // 2
    pair = jnp.where(lane < half, lane, lane - half).astype(F32)
    inv_freq = jnp.exp(pair * (-2.0 * math.log(ROPE_THETA) / d))
    ang = pos * inv_freq
    cos_ref[...] = jnp.cos(ang)
    sin_ref[...] = jnp.where(lane < half, -jnp.sin(ang), jnp.sin(ang))


def _rope_tables(t_len):
    return pl.pallas_call(
        _rope_table_kernel,
        out_shape=(jax.ShapeDtypeStruct((t_len, HEAD_DIM), F32),
                   jax.ShapeDtypeStruct((t_len, HEAD_DIM), F32)),
        name="rope_tables",
    )()


_TILE_GDN_V = 2
_TILE_GDN_Z = 3
_TILE_MOBA_Q = 4
_TILE_MOBA_K = 5
EPILOGUE_ROWS = 256


def _in_proj_kernel(x_ref, nw_ref, wa_ref, wb_ref, wba_ref, convw_ref, cos_ref, sin_ref, qkw_ref,
                    o_ref, ba_ref, h_ref, carry_ref, *, rows_per_seq):
    i = pl.program_id(0)
    j = pl.program_id(1)
    tm, tn = o_ref.shape
    hd = HEAD_DIM
    slab = min(EPILOGUE_ROWS, tm)

    @pl.when((i == 0) & (j == 0))
    def _():
        carry_ref[...] = jnp.zeros_like(carry_ref)

    if tm >= 4 * slab:
        edges = [0, slab // 2, *range(slab // 2 + slab, tm - slab // 2, slab), tm - slab // 2, tm]
    else:
        edges = list(range(0, tm + 1, slab))
    slabs = list(zip(edges[:-1], edges[1:]))

    @pl.when(j == 0)
    def _():
        x = x_ref[...]
        var = jnp.mean(x * x, axis=-1, keepdims=True)
        h = (x * lax.rsqrt(var + EPS) * nw_ref[...]).astype(BF16)
        h_ref[...] = h
        ba_ref[...] = jnp.dot(h, wba_ref[...], preferred_element_type=F32)

    def project(r0, r1, w_ref):
        return jnp.dot(h_ref[r0:r1, :], w_ref[...], preferred_element_type=F32)

    def plain(w_ref):
        for r0 in range(0, tm, slab):
            o_ref[r0:r0 + slab, :] = project(r0, r0 + slab, w_ref).astype(o_ref.dtype)

    @pl.when(j <= _TILE_GDN_V)
    def _():
        seq_start = (i * tm) % rows_per_seq == 0
        prev = jnp.where(seq_start, 0.0, carry_ref[j])
        wv = convw_ref[...]
        for r0, r1 in slabs:
            acc = project(r0, r1, wa_ref)
            window = jnp.concatenate([prev, acc], axis=0)
            y = acc * wv[GDN_CONV - 1:GDN_CONV, :]
            for s in range(1, GDN_CONV):
                y = y + (pltpu.roll(window, s, axis=0)[CARRY_ROWS:]
                         * wv[GDN_CONV - 1 - s:GDN_CONV - s, :])
            prev = acc[r1 - r0 - CARRY_ROWS:]
            y = _silu(y)
            for h in range(tn // hd):
                yh = y[:, h * hd:(h + 1) * hd]
                l2 = lax.rsqrt(jnp.sum(yh * yh, axis=-1, keepdims=True) + EPS)
                yh = yh * jnp.where(j < _TILE_GDN_V, l2, 1.0)
                o_ref[r0:r1, h * hd:(h + 1) * hd] = yh.astype(o_ref.dtype)
        carry_ref[j] = prev


    @pl.when(j == _TILE_GDN_Z)
    def _():
        plain(wa_ref)

    @pl.when((j == _TILE_MOBA_Q) | (j == _TILE_MOBA_K))
    def _():
        w = jnp.where(j == _TILE_MOBA_Q, qkw_ref[0:1, :], qkw_ref[1:2, :])
        for r0, r1 in slabs:
            acc = project(r0, r1, wb_ref)
            cos = cos_ref[r0:r1, :]
            sin = sin_ref[r0:r1, :]
            for h in range(tn // hd):
                x = acc[:, h * hd:(h + 1) * hd]
                y = x * lax.rsqrt(jnp.mean(x * x, axis=-1, keepdims=True) + EPS) * w
                y = y * cos + pltpu.roll(y, hd // 2, axis=1) * sin
                o_ref[r0:r1, h * hd:(h + 1) * hd] = y.astype(o_ref.dtype)

    @pl.when(j > _TILE_MOBA_K)
    def _():
        plain(wb_ref)


def _in_proj(x2, norm_w, w_a, w_b, w_ba, conv_w, cos_t, sin_t, qk_norm_w, tm, tn, t_len):
    m, d = x2.shape
    a_tiles = _TILE_GDN_Z + 1
    n = w_a.shape[1] + w_b.shape[1]
    assert t_len % tm == 0 and conv_w.shape[1] == (_TILE_GDN_V + 1) * tn
    assert w_a.shape[1] == a_tiles * tn and w_b.shape[1] % tn == 0
    b_tiles = w_b.shape[1] // tn
    seq_tiles = t_len // tm
    return pl.pallas_call(
        functools.partial(_in_proj_kernel, rows_per_seq=t_len),
        grid=(m // tm, n // tn),
        in_specs=[
            pl.BlockSpec((tm, d), lambda i, j: (i, 0)),
            pl.BlockSpec((1, d), lambda i, j: (0, 0)),
            pl.BlockSpec((d, tn), lambda i, j: (0, jnp.minimum(j, a_tiles - 1))),
            pl.BlockSpec((d, tn), lambda i, j: (0, jnp.where(j < a_tiles, b_tiles - 1, j - a_tiles))),
            pl.BlockSpec((d, HEAD_DIM), lambda i, j: (0, 0)),
            pl.BlockSpec((GDN_CONV, tn), lambda i, j: (0, jnp.minimum(j, _TILE_GDN_V))),
            pl.BlockSpec((tm, HEAD_DIM), lambda i, j: (i % seq_tiles, 0)),
            pl.BlockSpec((tm, HEAD_DIM), lambda i, j: (i % seq_tiles, 0)),
            pl.BlockSpec((2, HEAD_DIM), lambda i, j: (0, 0)),
        ],
        out_specs=(
            pl.BlockSpec((tm, tn), lambda i, j: (i, j)),
            pl.BlockSpec((tm, HEAD_DIM), lambda i, j: (i, 0)),
        ),
        out_shape=(jax.ShapeDtypeStruct((m, n), BF16),
                   jax.ShapeDtypeStruct((m, HEAD_DIM), F32)),
        scratch_shapes=[
            pltpu.VMEM((tm, d), BF16),
            pltpu.VMEM((_TILE_GDN_V + 1, CARRY_ROWS, tn), F32),
        ],
        compiler_params=pltpu.CompilerParams(
            dimension_semantics=("arbitrary", "arbitrary"), vmem_limit_bytes=VMEM_LIMIT),
        name="in_proj",
    )(x2, norm_w, w_a, w_b, w_ba, conv_w, cos_t, sin_t, qk_norm_w)


def _gdn_kernel(qkv_ref, z_ref, ba_ref, alog_ref, dtb_ref, nw_ref, o_ref, state_ref, *, n_heads):
    tt = qkv_ref.shape[0]
    c = GDN_CHUNK
    n_chunks = tt // c
    hd = HEAD_DIM
    width = n_heads * hd

    @pl.when(pl.program_id(1) == 0)
    def _():
        state_ref[...] = jnp.zeros_like(state_ref)

    ba = ba_ref[...]
    beta_all = _sigmoid(ba)
    sp_in = ba + dtb_ref[...]
    softplus = jnp.maximum(sp_in, 0.0) + jnp.log1p(jnp.exp(-jnp.abs(sp_in)))
    g_all = -jnp.exp(alog_ref[...]) * softplus

    row = lax.broadcasted_iota(jnp.int32, (tt, tt), 0)
    col = lax.broadcasted_iota(jnp.int32, (tt, tt), 1)
    same_chunk = (row // c) == (col // c)
    block_diag = same_chunk.astype(BF16)

    g_hi = g_all.astype(BF16)
    g_rest = g_all - g_hi.astype(F32)
    g_mid = g_rest.astype(BF16)
    g_lo = (g_rest - g_mid.astype(F32)).astype(BF16)
    sums = jnp.dot(jnp.concatenate([(same_chunk & (col <= row)).astype(BF16), block_diag], axis=0),
                   jnp.concatenate([g_hi, g_mid, g_lo], axis=1), preferred_element_type=F32)
    sums = sums[:, :hd] + sums[:, hd:2 * hd] + sums[:, 2 * hd:]
    cg_all = sums[:tt]
    cgl_all = sums[tt:]
    cgt_all = cg_all.T

    prow = lax.broadcasted_iota(jnp.int32, (c, tt), 0)
    plane = lax.broadcasted_iota(jnp.int32, (c, tt), 1)
    pcol = plane % c
    in_chunk = [plane // c == ci for ci in range(n_chunks)]
    causal_p = pcol <= prow
    strict_p = pcol < prow
    same16 = (prow // 16) == (pcol // 16)
    same32 = (prow // 32) == (pcol // 32)
    diag16 = strict_p & same16
    off16 = strict_p & same32 & jnp.logical_not(same16)
    off32 = strict_p & jnp.logical_not(same32)
    eye_p = (pcol == prow).astype(F32)

    def pack(full):
        out = full[(n_chunks - 1) * c:]
        for ci in range(n_chunks - 2, -1, -1):
            out = jnp.where(in_chunk[ci], full[ci * c:(ci + 1) * c], out)
        return out

    def block_diag_of(packed):
        return jnp.concatenate([packed.astype(BF16)] * n_chunks, axis=0) * block_diag

    def mm_packed(packed, bd):
        return jnp.dot(packed.astype(BF16), bd, preferred_element_type=F32)

    scale = hd ** -0.5
    nw = nw_ref[...]
    heads = range(n_heads)

    def each(fn, *per_head):
        return [fn(*args) for args in zip(*per_head)]

    kb = [qkv_ref[:, width + h * hd:width + (h + 1) * hd] for h in heads]
    q = [qkv_ref[:, h * hd:(h + 1) * hd].astype(F32) * scale for h in heads]
    k = each(lambda x: x.astype(F32), kb)
    v = [qkv_ref[:, 2 * width + h * hd:2 * width + (h + 1) * hd].astype(F32) for h in heads]
    beta = [beta_all[:, h:h + 1] for h in heads]
    cg = [cg_all[:, n_heads + h:n_heads + h + 1] for h in heads]
    cgl = [cgl_all[:, n_heads + h:n_heads + h + 1] for h in heads]
    cg_row = [cgt_all[n_heads + h:n_heads + h + 1, :] for h in heads]

    decay = each(lambda a, b: jnp.exp(jnp.where(causal_p, pack(a) - b, -jnp.inf)), cg, cg_row)
    kk = each(lambda x: pack(_mm_nt(x, x)), kb)
    qk = each(lambda x, y, d: block_diag_of(pack(_mm_nt(x, y)) * d), q, kb, decay)
    a_mat = each(lambda b, x, d: jnp.where(strict_p, pack(b) * x * d, 0.0), beta, kk, decay)

    n1 = each(lambda a: -jnp.where(diag16, a, 0.0), a_mat)
    inv = each(lambda n: eye_p + n, n1)
    power = n1
    power_bd = each(block_diag_of, power)
    for _ in range(3):
        power = each(mm_packed, power, power_bd)
        power_bd = each(block_diag_of, power)
        inv = each(lambda t, p: t + mm_packed(t, p), inv, power_bd)
    for off in (off16, off32):
        left = each(lambda t, a: mm_packed(t, block_diag_of(jnp.where(off, a, 0.0))), inv, a_mat)
        inv = each(lambda t, x: t - mm_packed(x, block_diag_of(t)), inv, left)

    ecg = each(jnp.exp, cg)
    rhs = each(lambda vv, kx, b, e: jnp.concatenate([vv * b, kx * (b * e)], axis=1),
               v, k, beta, ecg)
    uw = each(lambda t, x: jnp.dot(block_diag_of(t), x.astype(BF16), preferred_element_type=F32),
              inv, rhs)
    q_dec = each(lambda x, e: x * e, q, ecg)
    k_dec = each(lambda kx, a, b: (kx * jnp.exp(a - b)).astype(BF16), k, cgl, cg)
    g_last = each(jnp.exp, cgl)

    state = [state_ref[h] for h in heads]
    v_new = [[] for _ in heads]
    o_state = [[] for _ in heads]
    for ci in range(n_chunks):
        rs = slice(ci * c, (ci + 1) * c)
        r = each(lambda x, qd, s: _mm(jnp.concatenate([x[rs, hd:], qd[rs]], axis=0), s),
                 uw, q_dec, state)
        vn = each(lambda x, y: x[rs, :hd] - y[:c], uw, r)
        state = each(lambda s, g, kd, x: s * g[ci * c:ci * c + 1, :] + _mm_tn(kd[rs], x),
                     state, g_last, k_dec, vn)
        for h in heads:
            v_new[h].append(vn[h])
            o_state[h].append(r[h][c:])
    for h in heads:
        state_ref[h] = state[h]
    o = each(lambda os, x, vs: jnp.concatenate(os, axis=0) + _mm(x, jnp.concatenate(vs, axis=0)),
             o_state, qk, v_new)

    for h in heads:
        on = o[h] * lax.rsqrt(jnp.mean(o[h] * o[h], axis=-1, keepdims=True) + EPS) * nw
        zz = z_ref[:, h * hd:(h + 1) * hd].astype(F32)
        o_ref[:, h * hd:(h + 1) * hd] = (on * _silu(zz)).astype(o_ref.dtype)


def _gdn(proj, ba, alog_row, dtb_row, norm_w, b_sz, t_len, n_heads, z_block):
    width = n_heads * HEAD_DIM
    tt = min(GDN_TILE, t_len)
    n_t = t_len // tt
    assert t_len % tt == 0 and tt % GDN_CHUNK == 0
    return pl.pallas_call(
        functools.partial(_gdn_kernel, n_heads=n_heads),
        grid=(b_sz, n_t),
        in_specs=[
            pl.BlockSpec((tt, 3 * width), lambda b, t: (b * n_t + t, 0)),
            pl.BlockSpec((tt, width), lambda b, t: (b * n_t + t, z_block)),
            pl.BlockSpec((tt, HEAD_DIM), lambda b, t: (b * n_t + t, 0)),
            pl.BlockSpec((1, HEAD_DIM), lambda b, t: (0, 0)),
            pl.BlockSpec((1, HEAD_DIM), lambda b, t: (0, 0)),
            pl.BlockSpec((1, HEAD_DIM), lambda b, t: (0, 0)),
        ],
        out_specs=pl.BlockSpec((tt, width), lambda b, t: (b * n_t + t, 0)),
        out_shape=jax.ShapeDtypeStruct((b_sz * t_len, width), BF16),
        scratch_shapes=[pltpu.VMEM((n_heads, HEAD_DIM, HEAD_DIM), F32)],
        compiler_params=pltpu.CompilerParams(
            dimension_semantics=("parallel", "arbitrary"), vmem_limit_bytes=VMEM_LIMIT),
        name="gdn",
    )(proj, proj, ba, alog_row, dtb_row, norm_w)


def _moba_kernel(q_ref, k_ref, v_ref, z_ref, o_ref,
                 qaug_s, kaug_s, vt_s, kmean_s, m_s, l_s, acc_s, *, unroll, pairs):
    t_len, hd = q_ref.shape
    bs = MOBA_BLOCK
    n_blk = t_len // bs
    gate_rows = -(-n_blk // CARRY_ROWS) * CARRY_ROWS
    exp_scale = hd ** -0.5 * math.log2(math.e)

    def rows(i):
        return slice(i * bs, (i + 1) * bs)

    def softmax_weights(s, m):
        return jnp.exp2(((s - m) * exp_scale).astype(BF16))


    def prep_load(i):
        return (i, q_ref[rows(i), :], k_ref[rows(i), :], v_ref[rows(i), :])

    def prep_compute(i, q, k, v):
        lane = lax.broadcasted_iota(jnp.int32, (bs, hd), 1)
        k_aug = jnp.concatenate([k, (lane == i).astype(BF16)], axis=1)
        v_t = jnp.concatenate([v.astype(F32).T, jnp.ones((SUM_ROWS, bs), F32)], axis=0)
        return q, k_aug, jnp.mean(k.astype(F32), axis=0, keepdims=True), v_t.astype(BF16)

    def prep_store(i, q, k_aug, k_mean, v_t):
        qaug_s[rows(i), :hd] = q
        kaug_s[rows(i), :] = k_aug
        kmean_s[i:i + 1, :] = k_mean
        vt_s[i] = v_t

    def own_load(i):
        return (i, qaug_s[rows(i), :hd], kaug_s[rows(i), :hd], vt_s[i], kmean_s[...])

    def own_compute(i, qi, ki, vi_t, k_mean):
        blk = lax.broadcasted_iota(jnp.int32, (gate_rows, bs), 0)
        gate = jnp.where(blk < i, _mm_nt(k_mean, qi)[:gate_rows], -jnp.inf)
        sel = blk < 0
        for _ in range(MOBA_TOPK):
            best = jnp.max(gate, axis=0, keepdims=True)
            is_best = (gate == best) & (gate > -jnp.inf)
            pick = blk == jnp.min(jnp.where(is_best, blk, hd), axis=0, keepdims=True)
            sel = sel | pick
            gate = jnp.where(pick, -jnp.inf, gate)
        bias_t = jnp.concatenate([jnp.where(sel, 0.0, MASK_BIAS),
                                  jnp.full((hd - gate_rows, bs), MASK_BIAS, F32)], axis=0)

        key = lax.broadcasted_iota(jnp.int32, (bs, bs), 0)
        qry = lax.broadcasted_iota(jnp.int32, (bs, bs), 1)
        s = jnp.where(key <= qry, _mm_nt(ki, qi), -jnp.inf)
        m0 = jnp.max(s, axis=0, keepdims=True)
        p = softmax_weights(s, m0)
        l0 = jnp.sum(p.astype(F32), axis=0, keepdims=True)
        return bias_t.T.astype(BF16), m0, l0, jnp.dot(vi_t[:hd], p, preferred_element_type=F32)

    def own_store(i, bias, m0, l0, acc0):
        qaug_s[rows(i), hd:] = bias
        m_s[i] = m0
        l_s[i] = l0
        acc_s[i] = acc0

    def past_group(group):
        weights = []
        for i, j in group:
            s = _mm_nt(kaug_s[j * bs:(j + 1) * bs, :], qaug_s[i * bs:(i + 1) * bs, :])
            m_old = m_s[i]
            m_new = jnp.maximum(m_old, jnp.max(s, axis=0, keepdims=True))
            m_s[i] = m_new
            weights.append((jnp.exp2((m_old - m_new) * exp_scale), softmax_weights(s, m_new)))
        for (i, j), (alpha, p) in zip(group, weights):
            pv = jnp.dot(vt_s[j], p, preferred_element_type=F32)
            l_s[i] = alpha * l_s[i] + pv[hd:hd + 1]
            acc_s[i] = alpha * acc_s[i] + pv[:hd]

    def finish_load(i):
        return (i, acc_s[i], l_s[i], z_ref[rows(i), :])

    def finish_compute(i, acc, l_fin, z):
        zz = z.astype(F32)
        return (((acc / l_fin).T * _silu(zz)).astype(o_ref.dtype),)

    def finish_store(i, out):
        o_ref[rows(i), :] = out

    def run(load, compute, store, count):
        for t in range(0, count, unroll):
            loaded = [load(t + u) for u in range(unroll)]
            results = [compute(*vals) for vals in loaded]
            for vals, res in zip(loaded, results):
                store(vals[0], *res)

    kmean_s[...] = jnp.zeros_like(kmean_s)
    run(prep_load, prep_compute, prep_store, n_blk)
    run(own_load, own_compute, own_store, n_blk)
    for g in range(0, len(pairs), unroll):
        past_group(pairs[g:g + unroll])
    run(finish_load, finish_compute, finish_store, n_blk)


def _moba_pair_schedule(n_blk):
    for unroll in (8, 4, 2, 1):
        if n_blk % unroll:
            continue
        todo = {i: list(range(i)) for i in range(1, n_blk)}
        pairs = []
        while any(todo.values()):
            busiest = sorted((i for i in todo if todo[i]), key=lambda i: -len(todo[i]))[:unroll]
            if len(busiest) < unroll:
                break
            pairs += [(i, todo[i].pop()) for i in busiest]
        else:
            return unroll, pairs
    raise AssertionError("unroll == 1 always schedules")


def _moba(proj, b_sz, t_len, n_heads, q_block0, z_block0):
    width = n_heads * HEAD_DIM
    n_blk = t_len // MOBA_BLOCK
    unroll, pairs = _moba_pair_schedule(n_blk)

    def col(base):
        return lambda b, h: (b, base + h)

    return pl.pallas_call(
        functools.partial(_moba_kernel, unroll=unroll, pairs=tuple(pairs)),
        grid=(b_sz, n_heads),
        in_specs=[
            pl.BlockSpec((t_len, HEAD_DIM), col(q_block0)),
            pl.BlockSpec((t_len, HEAD_DIM), col(q_block0 + n_heads)),
            pl.BlockSpec((t_len, HEAD_DIM), col(q_block0 + 2 * n_heads)),
            pl.BlockSpec((t_len, HEAD_DIM), col(z_block0)),
        ],
        out_specs=pl.BlockSpec((t_len, HEAD_DIM), col(0)),
        scratch_shapes=[
            pltpu.VMEM((t_len, 2 * HEAD_DIM), BF16),
            pltpu.VMEM((t_len, 2 * HEAD_DIM), BF16),
            pltpu.VMEM((n_blk, HEAD_DIM + SUM_ROWS, MOBA_BLOCK), BF16),
            pltpu.VMEM((HEAD_DIM, HEAD_DIM), F32),
            pltpu.VMEM((n_blk, 1, MOBA_BLOCK), F32),
            pltpu.VMEM((n_blk, 1, MOBA_BLOCK), F32),
            pltpu.VMEM((n_blk, HEAD_DIM, MOBA_BLOCK), F32),
        ],
        out_shape=jax.ShapeDtypeStruct((b_sz * t_len, width), BF16),
        compiler_params=pltpu.CompilerParams(
            dimension_semantics=("parallel", "parallel"), vmem_limit_bytes=VMEM_LIMIT),
        name="moba",
    )(proj, proj, proj, proj)


def _merge_kernel(x_ref, oa_ref, ob_ref, ga_ref, gb_ref, wa_ref, wb_ref, wo_ref, o_ref):
    ya = jnp.dot(oa_ref[...], wa_ref[...], preferred_element_type=F32)
    yb = jnp.dot(ob_ref[...], wb_ref[...], preferred_element_type=F32)
    merged = (_sigmoid(ga_ref[...].astype(F32)) * ya + _sigmoid(gb_ref[...].astype(F32)) * yb)
    o_ref[...] = x_ref[...] + jnp.dot(merged.astype(BF16), wo_ref[...],
                                      preferred_element_type=F32)


def _merge(x2, oa, ob, proj, wa, wb, wo, tm, gate_block0):
    m, d = x2.shape
    wa_rows, wb_rows = wa.shape[0], wb.shape[0]
    resident = dict(pipeline_mode=pl.Buffered(1))
    return pl.pallas_call(
        _merge_kernel,
        grid=(m // tm,),
        in_specs=[
            pl.BlockSpec((tm, d), lambda i: (i, 0)),
            pl.BlockSpec((tm, wa_rows), lambda i: (i, 0)),
            pl.BlockSpec((tm, wb_rows), lambda i: (i, 0)),
            pl.BlockSpec((tm, d), lambda i: (i, gate_block0)),
            pl.BlockSpec((tm, d), lambda i: (i, gate_block0 + 1)),
            pl.BlockSpec((wa_rows, d), lambda i: (0, 0), **resident),
            pl.BlockSpec((wb_rows, d), lambda i: (0, 0), **resident),
            pl.BlockSpec((d, d), lambda i: (0, 0), **resident),
        ],
        out_specs=pl.BlockSpec((tm, d), lambda i: (i, 0)),
        out_shape=jax.ShapeDtypeStruct((m, d), F32),
        compiler_params=pltpu.CompilerParams(
            dimension_semantics=("parallel",), vmem_limit_bytes=VMEM_LIMIT),
        name="merge_out",
    )(x2, oa, ob, proj, proj, wa, wb, wo)


def _lane_row(vec, offset):
    n = vec.shape[0]
    return jnp.pad(vec.astype(F32), (offset, HEAD_DIM - offset - n)).reshape(1, HEAD_DIM)


def _layer(x2, b_sz, t_len, cos_t, sin_t, norm_w, w_in, conv_w, a_log, dt_bias, gdn_norm_w,
           q_norm_w, k_norm_w, w_out_gdn, w_out_moba, w_o):
    m, d = x2.shape
    gw = w_out_gdn.shape[0]
    mw = w_out_moba.shape[0]
    gh = a_log.shape[0]
    mh = mw // HEAD_DIM
    assert gw == gh * HEAD_DIM and 2 * gh <= HEAD_DIM
    assert gw == mw and (8 * gw) % d == 0 and (2 * d) % gw == 0

    c1 = 4 * gw
    c2 = c1 + 2 * gh
    w_a = w_in[:, :c1].astype(BF16)
    w_b = w_in[:, c2:].astype(BF16)
    w_ba = jnp.pad(w_in[:, c1:c2], ((0, 0), (0, HEAD_DIM - 2 * gh))).astype(BF16)
    gdn_z_block = 3
    moba_q_block = _TILE_MOBA_Q * gw // HEAD_DIM
    moba_z_block = 7 * gw // HEAD_DIM
    gate_block = 8 * gw // d

    tm = math.gcd(1024, t_len)
    qk_norm_w = jnp.stack([q_norm_w, k_norm_w]).astype(F32)
    proj, ba = _in_proj(x2, norm_w.reshape(1, d), w_a, w_b, w_ba, conv_w, cos_t, sin_t, qk_norm_w,
                        tm, gw, t_len)

    oa = _gdn(proj, ba, _lane_row(a_log, gh), _lane_row(dt_bias, gh),
              gdn_norm_w.reshape(1, HEAD_DIM), b_sz, t_len, gh, gdn_z_block)
    ob = _moba(proj, b_sz, t_len, mh, moba_q_block, moba_z_block)
    return _merge(x2, oa, ob, proj, w_out_gdn.astype(BF16), w_out_moba.astype(BF16),
                  w_o.astype(BF16), min(256, m), gate_block)


def kernel(x, norm_w, w_in, gdn_conv_w, gdn_a_log, gdn_dt_bias, gdn_norm_w, moba_q_norm_w,
           moba_k_norm_w, w_out_gdn, w_out_moba, w_o):
    b_sz, t_len, d = x.shape
    cos_t, sin_t = _rope_tables(t_len)
    x2 = x.reshape(b_sz * t_len, d)
    for l in range(norm_w.shape[0]):
        x2 = _layer(x2, b_sz, t_len, cos_t, sin_t, norm_w[l], w_in[l], gdn_conv_w[l],
                    gdn_a_log[l], gdn_dt_bias[l], gdn_norm_w[l], moba_q_norm_w[l],
                    moba_k_norm_w[l], w_out_gdn[l], w_out_moba[l], w_o[l])
    return x2.reshape(b_sz, t_len, d)
```

```python
import functools
import math

import jax
import jax.numpy as jnp
from jax import lax
from jax.experimental import pallas as pl
from jax.experimental.pallas import tpu as pltpu

F32 = jnp.float32
BF16 = jnp.bfloat16

EPS = 1e-6
ROPE_THETA = 10000.0
HEAD_DIM = 128
GDN_CONV = 4
GDN_CHUNK = 64
GDN_TILE = 256
MOBA_BLOCK = 256
MOBA_TOPK = 3
MOBA_PREP_GROUP = 8
MOBA_ATTEND_GROUP = 8
MASK_BIAS = -1e30
CARRY_ROWS = 8
SUM_ROWS = 16
VMEM_LIMIT = 56 * 1024 * 1024


def _mm(a, b):
    return jnp.dot(a.astype(BF16), b.astype(BF16), preferred_element_type=F32)


def _mm_nt(a, b):
    return lax.dot_general(a.astype(BF16), b.astype(BF16), (((1,), (1,)), ((), ())),
                           preferred_element_type=F32)


def _mm_tn(a, b):
    return lax.dot_general(a.astype(BF16), b.astype(BF16), (((0,), (0,)), ((), ())),
                           preferred_element_type=F32)


def _sigmoid(x):
    return 0.5 * jnp.tanh(0.5 * x) + 0.5


def _silu(x):
    h = 0.5 * x
    return h + h * jnp.tanh(h)


def _rope_table_kernel(cos_ref, sin_ref):
    t_len, d = cos_ref.shape
    lane = lax.broadcasted_iota(jnp.int32, (t_len, d), 1)
    pos = lax.broadcasted_iota(jnp.int32, (t_len, d), 0).astype(F32)
    half = d // 2
    pair = jnp.where(lane < half, lane, lane - half).astype(F32)
    inv_freq = jnp.exp(pair * (-2.0 * math.log(ROPE_THETA) / d))
    ang = pos * inv_freq
    cos_ref[...] = jnp.cos(ang)
    sin_ref[...] = jnp.where(lane < half, -jnp.sin(ang), jnp.sin(ang))


def _rope_tables(t_len):
    return pl.pallas_call(
        _rope_table_kernel,
        out_shape=(jax.ShapeDtypeStruct((t_len, HEAD_DIM), F32),
                   jax.ShapeDtypeStruct((t_len, HEAD_DIM), F32)),
        name="rope_tables",
    )()


_TILE_GDN_V = 2
_TILE_GDN_Z = 3
_TILE_MOBA_Q = 4
_TILE_MOBA_K = 5
EPILOGUE_ROWS = 256


def _in_proj_kernel(x_ref, nw_ref, wa_ref, wb_ref, wba_ref, convw_ref, cos_ref, sin_ref, qkw_ref,
                    o_ref, ba_ref, h_ref, carry_ref, *, rows_per_seq):
    i = pl.program_id(0)
    j = pl.program_id(1)
    tm, tn = o_ref.shape
    hd = HEAD_DIM
    slab = min(EPILOGUE_ROWS, tm)

    @pl.when((i == 0) & (j == 0))
    def _():
        carry_ref[...] = jnp.zeros_like(carry_ref)

    if tm >= 4 * slab:
        edges = [0, slab // 2, *range(slab // 2 + slab, tm - slab // 2, slab), tm - slab // 2, tm]
    else:
        edges = list(range(0, tm + 1, slab))
    slabs = list(zip(edges[:-1], edges[1:]))

    @pl.when(j == 0)
    def _():
        x = x_ref[...]
        var = jnp.mean(x * x, axis=-1, keepdims=True)
        h = (x * lax.rsqrt(var + EPS) * nw_ref[...]).astype(BF16)
        h_ref[...] = h
        ba_ref[...] = jnp.dot(h, wba_ref[...], preferred_element_type=F32)

    def project(r0, r1, w_ref):
        return jnp.dot(h_ref[r0:r1, :], w_ref[...], preferred_element_type=F32)

    def plain(w_ref):
        for r0 in range(0, tm, slab):
            o_ref[r0:r0 + slab, :] = project(r0, r0 + slab, w_ref).astype(o_ref.dtype)

    @pl.when(j <= _TILE_GDN_V)
    def _():
        seq_start = (i * tm) % rows_per_seq == 0
        prev = jnp.where(seq_start, 0.0, carry_ref[j])
        wv = convw_ref[...]
        for r0, r1 in slabs:
            acc = project(r0, r1, wa_ref)
            window = jnp.concatenate([prev, acc], axis=0)
            shifted = pltpu.roll(window, 1, axis=0)
            far = window * wv[1:2, :] + shifted * wv[0:1, :]
            y = (acc * wv[3:4, :] + shifted[CARRY_ROWS:] * wv[2:3, :]
                 + pltpu.roll(far, 2, axis=0)[CARRY_ROWS:])
            prev = acc[r1 - r0 - CARRY_ROWS:]
            y = _silu(y)
            for h in range(tn // hd):
                yh = y[:, h * hd:(h + 1) * hd]
                l2 = lax.rsqrt(jnp.sum(yh * yh, axis=-1, keepdims=True) + EPS)
                yh = yh * jnp.where(j < _TILE_GDN_V, l2, 1.0)
                o_ref[r0:r1, h * hd:(h + 1) * hd] = yh.astype(o_ref.dtype)
        carry_ref[j] = prev


    @pl.when(j == _TILE_GDN_Z)
    def _():
        plain(wa_ref)

    @pl.when((j == _TILE_MOBA_Q) | (j == _TILE_MOBA_K))
    def _():
        w = jnp.where(j == _TILE_MOBA_Q, qkw_ref[0:1, :], qkw_ref[1:2, :])
        for r0, r1 in slabs:
            acc = project(r0, r1, wb_ref)
            cos = cos_ref[r0:r1, :]
            sin = sin_ref[r0:r1, :]
            for h in range(tn // hd):
                x = acc[:, h * hd:(h + 1) * hd]
                y = x * lax.rsqrt(jnp.mean(x * x, axis=-1, keepdims=True) + EPS) * w
                y = y * cos + pltpu.roll(y, hd // 2, axis=1) * sin
                o_ref[r0:r1, h * hd:(h + 1) * hd] = y.astype(o_ref.dtype)

    @pl.when(j > _TILE_MOBA_K)
    def _():
        plain(wb_ref)


def _in_proj(x2, norm_w, w_a, w_b, w_ba, conv_w, cos_t, sin_t, qk_norm_w, tm, tn, t_len):
    m, d = x2.shape
    a_tiles = _TILE_GDN_Z + 1
    n = w_a.shape[1] + w_b.shape[1]
    assert t_len % tm == 0 and conv_w.shape == (GDN_CONV, (_TILE_GDN_V + 1) * tn) and GDN_CONV == 4
    assert w_a.shape[1] == a_tiles * tn and w_b.shape[1] % tn == 0
    b_tiles = w_b.shape[1] // tn
    seq_tiles = t_len // tm
    return pl.pallas_call(
        functools.partial(_in_proj_kernel, rows_per_seq=t_len),
        grid=(m // tm, n // tn),
        in_specs=[
            pl.BlockSpec((tm, d), lambda i, j: (i, 0)),
            pl.BlockSpec((1, d), lambda i, j: (0, 0)),
            pl.BlockSpec((d, tn), lambda i, j: (0, jnp.minimum(j, a_tiles - 1))),
            pl.BlockSpec((d, tn), lambda i, j: (0, jnp.where(j < a_tiles, b_tiles - 1, j - a_tiles))),
            pl.BlockSpec((d, HEAD_DIM), lambda i, j: (0, 0)),
            pl.BlockSpec((GDN_CONV, tn), lambda i, j: (0, jnp.minimum(j, _TILE_GDN_V))),
            pl.BlockSpec((tm, HEAD_DIM), lambda i, j: (i % seq_tiles, 0)),
            pl.BlockSpec((tm, HEAD_DIM), lambda i, j: (i % seq_tiles, 0)),
            pl.BlockSpec((2, HEAD_DIM), lambda i, j: (0, 0)),
        ],
        out_specs=(
            pl.BlockSpec((tm, tn), lambda i, j: (i, j)),
            pl.BlockSpec((tm, HEAD_DIM), lambda i, j: (i, 0)),
        ),
        out_shape=(jax.ShapeDtypeStruct((m, n), BF16),
                   jax.ShapeDtypeStruct((m, HEAD_DIM), F32)),
        scratch_shapes=[
            pltpu.VMEM((tm, d), BF16),
            pltpu.VMEM((_TILE_GDN_V + 1, CARRY_ROWS, tn), F32),
        ],
        compiler_params=pltpu.CompilerParams(
            dimension_semantics=("arbitrary", "arbitrary"), vmem_limit_bytes=VMEM_LIMIT),
        name="in_proj",
    )(x2, norm_w, w_a, w_b, w_ba, conv_w, cos_t, sin_t, qk_norm_w)


def _gdn_kernel(qkv_ref, z_ref, ba_ref, alog_ref, dtb_ref, nw_ref, o_ref, state_ref, *, n_heads):
    tt = qkv_ref.shape[0]
    c = GDN_CHUNK
    n_chunks = tt // c
    hd = HEAD_DIM
    width = n_heads * hd

    @pl.when(pl.program_id(1) == 0)
    def _():
        state_ref[...] = jnp.zeros_like(state_ref)

    ba = ba_ref[...]
    beta_all = _sigmoid(ba)
    sp_in = ba + dtb_ref[...]
    softplus = jnp.maximum(sp_in, 0.0) + jnp.log1p(jnp.exp(-jnp.abs(sp_in)))
    g_all = -jnp.exp(alog_ref[...]) * softplus

    row = lax.broadcasted_iota(jnp.int32, (tt, tt), 0)
    col = lax.broadcasted_iota(jnp.int32, (tt, tt), 1)
    same_chunk = (row // c) == (col // c)
    block_diag = same_chunk.astype(BF16)

    g_hi = g_all.astype(BF16)
    g_rest = g_all - g_hi.astype(F32)
    g_mid = g_rest.astype(BF16)
    g_lo = (g_rest - g_mid.astype(F32)).astype(BF16)
    sums = jnp.dot(jnp.concatenate([(same_chunk & (col <= row)).astype(BF16), block_diag], axis=0),
                   jnp.concatenate([g_hi, g_mid, g_lo], axis=1), preferred_element_type=F32)
    sums = sums[:, :hd] + sums[:, hd:2 * hd] + sums[:, 2 * hd:]
    cg_all = sums[:tt]
    cgl_all = sums[tt:]
    cgt_all = cg_all.T

    prow = lax.broadcasted_iota(jnp.int32, (c, tt), 0)
    plane = lax.broadcasted_iota(jnp.int32, (c, tt), 1)
    pcol = plane % c
    in_chunk = [plane // c == ci for ci in range(n_chunks)]
    causal_p = pcol <= prow
    strict_p = pcol < prow
    same16 = (prow // 16) == (pcol // 16)
    same32 = (prow // 32) == (pcol // 32)
    diag16 = strict_p & same16
    off16 = strict_p & same32 & jnp.logical_not(same16)
    off32 = strict_p & jnp.logical_not(same32)
    eye_p = (pcol == prow).astype(F32)

    def pack(full):
        out = full[(n_chunks - 1) * c:]
        for ci in range(n_chunks - 2, -1, -1):
            out = jnp.where(in_chunk[ci], full[ci * c:(ci + 1) * c], out)
        return out

    def block_diag_of(packed):
        return jnp.concatenate([packed.astype(BF16)] * n_chunks, axis=0) * block_diag

    def mm_packed(packed, bd):
        return jnp.dot(packed.astype(BF16), bd, preferred_element_type=F32)

    scale = hd ** -0.5
    nw = nw_ref[...]
    heads = range(n_heads)

    def each(fn, *per_head):
        return [fn(*args) for args in zip(*per_head)]

    kb = [qkv_ref[:, width + h * hd:width + (h + 1) * hd] for h in heads]
    q = [qkv_ref[:, h * hd:(h + 1) * hd].astype(F32) * scale for h in heads]
    k = each(lambda x: x.astype(F32), kb)
    v = [qkv_ref[:, 2 * width + h * hd:2 * width + (h + 1) * hd].astype(F32) for h in heads]
    beta = [beta_all[:, h:h + 1] for h in heads]
    cg = [cg_all[:, n_heads + h:n_heads + h + 1] for h in heads]
    cgl = [cgl_all[:, n_heads + h:n_heads + h + 1] for h in heads]
    cg_row = [cgt_all[n_heads + h:n_heads + h + 1, :] for h in heads]

    decay = each(lambda a, b: jnp.exp(jnp.where(causal_p, pack(a) - b, -jnp.inf)), cg, cg_row)
    kk = each(lambda x: pack(_mm_nt(x, x)), kb)
    qk = each(lambda x, y, d: block_diag_of(pack(_mm_nt(x, y)) * d), q, kb, decay)
    a_mat = each(lambda b, x, d: jnp.where(strict_p, pack(b) * x * d, 0.0), beta, kk, decay)

    n1 = each(lambda a: -jnp.where(diag16, a, 0.0), a_mat)
    inv = each(lambda n: eye_p + n, n1)
    power = n1
    power_bd = each(block_diag_of, power)
    for _ in range(3):
        power = each(mm_packed, power, power_bd)
        power_bd = each(block_diag_of, power)
        inv = each(lambda t, p: t + mm_packed(t, p), inv, power_bd)
    for off in (off16, off32):
        left = each(lambda t, a: mm_packed(t, block_diag_of(jnp.where(off, a, 0.0))), inv, a_mat)
        inv = each(lambda t, x: t - mm_packed(x, block_diag_of(t)), inv, left)

    ecg = each(jnp.exp, cg)
    rhs = each(lambda vv, kx, b, e: jnp.concatenate([vv * b, kx * (b * e)], axis=1),
               v, k, beta, ecg)
    uw = each(lambda t, x: jnp.dot(block_diag_of(t), x.astype(BF16), preferred_element_type=F32),
              inv, rhs)
    q_dec = each(lambda x, e: x * e, q, ecg)
    k_dec = each(lambda kx, a, b: (kx * jnp.exp(a - b)).astype(BF16), k, cgl, cg)
    g_last = each(jnp.exp, cgl)

    state = [state_ref[h] for h in heads]
    v_new = [[] for _ in heads]
    o_state = [[] for _ in heads]
    for ci in range(n_chunks):
        rs = slice(ci * c, (ci + 1) * c)
        r = each(lambda x, qd, s: _mm(jnp.concatenate([x[rs, hd:], qd[rs]], axis=0), s),
                 uw, q_dec, state)
        vn = each(lambda x, y: x[rs, :hd] - y[:c], uw, r)
        state = each(lambda s, g, kd, x: s * g[ci * c:ci * c + 1, :] + _mm_tn(kd[rs], x),
                     state, g_last, k_dec, vn)
        for h in heads:
            v_new[h].append(vn[h])
            o_state[h].append(r[h][c:])
    for h in heads:
        state_ref[h] = state[h]
    o = each(lambda os, x, vs: jnp.concatenate(os, axis=0) + _mm(x, jnp.concatenate(vs, axis=0)),
             o_state, qk, v_new)

    for h in heads:
        on = o[h] * lax.rsqrt(jnp.mean(o[h] * o[h], axis=-1, keepdims=True) + EPS) * nw
        zz = z_ref[:, h * hd:(h + 1) * hd].astype(F32)
        o_ref[:, h * hd:(h + 1) * hd] = (on * _silu(zz)).astype(o_ref.dtype)


def _gdn(proj, ba, alog_row, dtb_row, norm_w, b_sz, t_len, n_heads, z_block):
    width = n_heads * HEAD_DIM
    tt = min(GDN_TILE, t_len)
    n_t = t_len // tt
    assert t_len % tt == 0 and tt % GDN_CHUNK == 0
    return pl.pallas_call(
        functools.partial(_gdn_kernel, n_heads=n_heads),
        grid=(b_sz, n_t),
        in_specs=[
            pl.BlockSpec((tt, 3 * width), lambda b, t: (b * n_t + t, 0)),
            pl.BlockSpec((tt, width), lambda b, t: (b * n_t + t, z_block)),
            pl.BlockSpec((tt, HEAD_DIM), lambda b, t: (b * n_t + t, 0)),
            pl.BlockSpec((1, HEAD_DIM), lambda b, t: (0, 0)),
            pl.BlockSpec((1, HEAD_DIM), lambda b, t: (0, 0)),
            pl.BlockSpec((1, HEAD_DIM), lambda b, t: (0, 0)),
        ],
        out_specs=pl.BlockSpec((tt, width), lambda b, t: (b * n_t + t, 0)),
        out_shape=jax.ShapeDtypeStruct((b_sz * t_len, width), BF16),
        scratch_shapes=[pltpu.VMEM((n_heads, HEAD_DIM, HEAD_DIM), F32)],
        compiler_params=pltpu.CompilerParams(
            dimension_semantics=("parallel", "arbitrary"), vmem_limit_bytes=VMEM_LIMIT),
        name="gdn",
    )(proj, proj, ba, alog_row, dtb_row, norm_w)


def _moba_kernel(q_ref, k_ref, v_ref, z_ref, o_ref, qaug_s, kaug_s, vt_s, kmean_s, *, unroll):
    t_len, hd = q_ref.shape
    bs = MOBA_BLOCK
    n_blk = t_len // bs
    gate_rows = -(-n_blk // CARRY_ROWS) * CARRY_ROWS
    exp_scale = hd ** -0.5 * math.log2(math.e)

    def rows(i):
        return slice(i * bs, (i + 1) * bs)

    def softmax_weights(s, m):
        return jnp.exp2(((s - m) * exp_scale).astype(BF16))


    def prep_load(i):
        return (i, q_ref[rows(i), :], k_ref[rows(i), :], v_ref[rows(i), :])

    def prep_compute(i, q, k, v):
        lane = lax.broadcasted_iota(jnp.int32, (bs, hd), 1)
        k_aug = jnp.concatenate([k, (lane == i).astype(BF16)], axis=1)
        v_t = jnp.concatenate([v.astype(F32).T, jnp.ones((SUM_ROWS, bs), F32)], axis=0)
        return q, k_aug, jnp.mean(k.astype(F32), axis=0, keepdims=True), v_t.astype(BF16)

    def prep_store(i, q, k_aug, k_mean, v_t):
        qaug_s[rows(i), :hd] = q
        kaug_s[rows(i), :] = k_aug
        kmean_s[i:i + 1, :] = k_mean
        vt_s[:, rows(i)] = v_t

    def gate_load(i):
        return (i, qaug_s[rows(i), :hd], kmean_s[...])

    def gate_compute(i, qi, k_mean):
        blk = lax.broadcasted_iota(jnp.int32, (gate_rows, bs), 0)
        gate = jnp.where(blk < i, _mm_nt(k_mean, qi)[:gate_rows], -jnp.inf)
        sel = blk == i
        for _ in range(MOBA_TOPK):
            best = jnp.max(gate, axis=0, keepdims=True)
            is_best = (gate == best) & (gate > -jnp.inf)
            pick = blk == jnp.min(jnp.where(is_best, blk, hd), axis=0, keepdims=True)
            sel = sel | pick
            gate = jnp.where(pick, -jnp.inf, gate)
        bias_t = jnp.concatenate([jnp.where(sel, 0.0, MASK_BIAS),
                                  jnp.full((hd - gate_rows, bs), MASK_BIAS, F32)], axis=0)
        return (bias_t.T.astype(BF16),)

    def gate_store(i, bias):
        qaug_s[rows(i), hd:] = bias

    def attend_load(i):
        keys = slice(0, (i + 1) * bs)
        return (i, kaug_s[keys, :], qaug_s[rows(i), :], vt_s[:, keys], z_ref[rows(i), :])

    def attend(blocks):
        loaded = [attend_load(i) for i in blocks]
        scores = [_mm_nt(k_aug, q_aug) for _, k_aug, q_aug, _, _ in loaded]
        key = lax.broadcasted_iota(jnp.int32, (bs, bs), 0)
        qry = lax.broadcasted_iota(jnp.int32, (bs, bs), 1)
        weights = []
        for i, s in zip(blocks, scores):
            own = jnp.where(key <= qry, s[i * bs:], -jnp.inf)
            s = jnp.concatenate([s[:i * bs], own], axis=0) if i else own
            weights.append(softmax_weights(s, jnp.max(s, axis=0, keepdims=True)))
        applied = [jnp.dot(v_t, p, preferred_element_type=F32)
                   for (_, _, _, v_t, _), p in zip(loaded, weights)]
        for (i, _, _, _, z), pv in zip(loaded, applied):
            out = (pv[:hd] / pv[hd:hd + 1]).T * _silu(z.astype(F32))
            o_ref[rows(i), :] = out.astype(o_ref.dtype)

    def run(load, compute, store, blocks, group):
        for t in range(0, len(blocks), group):
            loaded = [load(i) for i in blocks[t:t + group]]
            results = [compute(*vals) for vals in loaded]
            for vals, res in zip(loaded, results):
                store(vals[0], *res)

    in_order = list(range(n_blk))
    balanced = [i for pair in zip(reversed(in_order), in_order) for i in pair][:n_blk]
    kmean_s[...] = jnp.zeros_like(kmean_s)
    run(prep_load, prep_compute, prep_store, in_order, unroll)
    run(gate_load, gate_compute, gate_store, in_order, unroll)
    for t in range(0, n_blk, MOBA_ATTEND_GROUP):
        attend(balanced[t:t + MOBA_ATTEND_GROUP])


def _moba(proj, b_sz, t_len, n_heads, q_block0, z_block0):
    width = n_heads * HEAD_DIM
    n_blk = t_len // MOBA_BLOCK
    assert t_len % MOBA_BLOCK == 0 and n_blk <= HEAD_DIM

    def col(base):
        return lambda b, h: (b, base + h)

    return pl.pallas_call(
        functools.partial(_moba_kernel, unroll=MOBA_PREP_GROUP),
        grid=(b_sz, n_heads),
        in_specs=[
            pl.BlockSpec((t_len, HEAD_DIM), col(q_block0)),
            pl.BlockSpec((t_len, HEAD_DIM), col(q_block0 + n_heads)),
            pl.BlockSpec((t_len, HEAD_DIM), col(q_block0 + 2 * n_heads)),
            pl.BlockSpec((t_len, HEAD_DIM), col(z_block0)),
        ],
        out_specs=pl.BlockSpec((t_len, HEAD_DIM), col(0)),
        scratch_shapes=[
            pltpu.VMEM((t_len, 2 * HEAD_DIM), BF16),
            pltpu.VMEM((t_len, 2 * HEAD_DIM), BF16),
            pltpu.VMEM((HEAD_DIM + SUM_ROWS, t_len), BF16),
            pltpu.VMEM((HEAD_DIM, HEAD_DIM), F32),
        ],
        out_shape=jax.ShapeDtypeStruct((b_sz * t_len, width), BF16),
        compiler_params=pltpu.CompilerParams(
            dimension_semantics=("parallel", "parallel"), vmem_limit_bytes=VMEM_LIMIT),
        name="moba",
    )(proj, proj, proj, proj)


def _merge_kernel(x_ref, oa_ref, ob_ref, ga_ref, gb_ref, wa_ref, wb_ref, wo_ref, o_ref):
    ya = jnp.dot(oa_ref[...], wa_ref[...], preferred_element_type=F32)
    yb = jnp.dot(ob_ref[...], wb_ref[...], preferred_element_type=F32)
    merged = (_sigmoid(ga_ref[...].astype(F32)) * ya + _sigmoid(gb_ref[...].astype(F32)) * yb)
    o_ref[...] = x_ref[...] + jnp.dot(merged.astype(BF16), wo_ref[...],
                                      preferred_element_type=F32)


def _merge(x2, oa, ob, proj, wa, wb, wo, tm, gate_block0):
    m, d = x2.shape
    wa_rows, wb_rows = wa.shape[0], wb.shape[0]
    resident = dict(pipeline_mode=pl.Buffered(1))
    return pl.pallas_call(
        _merge_kernel,
        grid=(m // tm,),
        in_specs=[
            pl.BlockSpec((tm, d), lambda i: (i, 0)),
            pl.BlockSpec((tm, wa_rows), lambda i: (i, 0)),
            pl.BlockSpec((tm, wb_rows), lambda i: (i, 0)),
            pl.BlockSpec((tm, d), lambda i: (i, gate_block0)),
            pl.BlockSpec((tm, d), lambda i: (i, gate_block0 + 1)),
            pl.BlockSpec((wa_rows, d), lambda i: (0, 0), **resident),
            pl.BlockSpec((wb_rows, d), lambda i: (0, 0), **resident),
            pl.BlockSpec((d, d), lambda i: (0, 0), **resident),
        ],
        out_specs=pl.BlockSpec((tm, d), lambda i: (i, 0)),
        out_shape=jax.ShapeDtypeStruct((m, d), F32),
        compiler_params=pltpu.CompilerParams(
            dimension_semantics=("parallel",), vmem_limit_bytes=VMEM_LIMIT),
        name="merge_out",
    )(x2, oa, ob, proj, proj, wa, wb, wo)


def _lane_row(vec, offset):
    n = vec.shape[0]
    return jnp.pad(vec.astype(F32), (offset, HEAD_DIM - offset - n)).reshape(1, HEAD_DIM)


def _layer(x2, b_sz, t_len, cos_t, sin_t, norm_w, w_in, conv_w, a_log, dt_bias, gdn_norm_w,
           q_norm_w, k_norm_w, w_out_gdn, w_out_moba, w_o):
    m, d = x2.shape
    gw = w_out_gdn.shape[0]
    mw = w_out_moba.shape[0]
    gh = a_log.shape[0]
    mh = mw // HEAD_DIM
    assert gw == gh * HEAD_DIM and 2 * gh <= HEAD_DIM
    assert gw == mw and (8 * gw) % d == 0 and (2 * d) % gw == 0

    c1 = 4 * gw
    c2 = c1 + 2 * gh
    w_a = w_in[:, :c1].astype(BF16)
    w_b = w_in[:, c2:].astype(BF16)
    w_ba = jnp.pad(w_in[:, c1:c2], ((0, 0), (0, HEAD_DIM - 2 * gh))).astype(BF16)
    gdn_z_block = 3
    moba_q_block = _TILE_MOBA_Q * gw // HEAD_DIM
    moba_z_block = 7 * gw // HEAD_DIM
    gate_block = 8 * gw // d

    tm = math.gcd(1024, t_len)
    qk_norm_w = jnp.stack([q_norm_w, k_norm_w]).astype(F32)
    proj, ba = _in_proj(x2, norm_w.reshape(1, d), w_a, w_b, w_ba, conv_w, cos_t, sin_t, qk_norm_w,
                        tm, gw, t_len)

    oa = _gdn(proj, ba, _lane_row(a_log, gh), _lane_row(dt_bias, gh),
              gdn_norm_w.reshape(1, HEAD_DIM), b_sz, t_len, gh, gdn_z_block)
    ob = _moba(proj, b_sz, t_len, mh, moba_q_block, moba_z_block)
    return _merge(x2, oa, ob, proj, w_out_gdn.astype(BF16), w_out_moba.astype(BF16),
                  w_o.astype(BF16), min(256, m), gate_block)


def kernel(x, norm_w, w_in, gdn_conv_w, gdn_a_log, gdn_dt_bias, gdn_norm_w, moba_q_norm_w,
           moba_k_norm_w, w_out_gdn, w_out_moba, w_o):
    b_sz, t_len, d = x.shape
    cos_t, sin_t = _rope_tables(t_len)
    x2 = x.reshape(b_sz * t_len, d)
    for l in range(norm_w.shape[0]):
        x2 = _layer(x2, b_sz, t_len, cos_t, sin_t, norm_w[l], w_in[l], gdn_conv_w[l],
                    gdn_a_log[l], gdn_dt_bias[l], gdn_norm_w[l], moba_q_norm_w[l],
                    moba_k_norm_w[l], w_out_gdn[l], w_out_moba[l], w_o[l])
    return x2.reshape(b_sz, t_len, d)
```

```python
import functools
import math

import jax
import jax.numpy as jnp
from jax import lax
from jax.experimental import pallas as pl
from jax.experimental.pallas import tpu as pltpu

F32 = jnp.float32
BF16 = jnp.bfloat16

EPS = 1e-6
ROPE_THETA = 10000.0
HEAD_DIM = 128
GDN_CONV = 4
GDN_CHUNK = 64
GDN_TILE = 256
MOBA_BLOCK = 256
MOBA_TOPK = 3
MOBA_PREP_GROUP = 8
MOBA_ATTEND_GROUP = 8
MASK_BIAS = -1e30
MERGE_ROWS = 512
CARRY_ROWS = 8
SUM_ROWS = 16
VMEM_LIMIT = 56 * 1024 * 1024


def _mm(a, b):
    return jnp.dot(a.astype(BF16), b.astype(BF16), preferred_element_type=F32)


def _mm_nt(a, b):
    return lax.dot_general(a.astype(BF16), b.astype(BF16), (((1,), (1,)), ((), ())),
                           preferred_element_type=F32)


def _mm_tn(a, b):
    return lax.dot_general(a.astype(BF16), b.astype(BF16), (((0,), (0,)), ((), ())),
                           preferred_element_type=F32)


def _sigmoid(x):
    return 0.5 * jnp.tanh(0.5 * x) + 0.5


def _silu(x):
    h = 0.5 * x
    return h + h * jnp.tanh(h)


def _rope_table_kernel(cos_ref, sin_ref):
    t_len, d = cos_ref.shape
    lane = lax.broadcasted_iota(jnp.int32, (t_len, d), 1)
    pos = lax.broadcasted_iota(jnp.int32, (t_len, d), 0).astype(F32)
    half = d // 2
    pair = jnp.where(lane < half, lane, lane - half).astype(F32)
    inv_freq = jnp.exp(pair * (-2.0 * math.log(ROPE_THETA) / d))
    ang = pos * inv_freq
    cos_ref[...] = jnp.cos(ang)
    sin_ref[...] = jnp.where(lane < half, -jnp.sin(ang), jnp.sin(ang))


def _rope_tables(t_len):
    return pl.pallas_call(
        _rope_table_kernel,
        out_shape=(jax.ShapeDtypeStruct((t_len, HEAD_DIM), F32),
                   jax.ShapeDtypeStruct((t_len, HEAD_DIM), F32)),
        name="rope_tables",
    )()


_TILE_GDN_V = 2
_TILE_GDN_Z = 3
_TILE_MOBA_Q = 4
_TILE_MOBA_K = 5
EPILOGUE_ROWS = 256


def _in_proj_kernel(x_ref, nw_ref, wa_ref, wb_ref, wba_ref, convw_ref, cos_ref, sin_ref, qkw_ref,
                    o_ref, ba_ref, h_ref, carry_ref, *, rows_per_seq):
    i = pl.program_id(0)
    j = pl.program_id(1)
    tm, tn = o_ref.shape
    hd = HEAD_DIM
    slab = min(EPILOGUE_ROWS, tm)

    @pl.when((i == 0) & (j == 0))
    def _():
        carry_ref[...] = jnp.zeros_like(carry_ref)

    if tm >= 4 * slab:
        edges = [0, slab // 2, *range(slab // 2 + slab, tm - slab // 2, slab), tm - slab // 2, tm]
    else:
        edges = list(range(0, tm + 1, slab))
    slabs = list(zip(edges[:-1], edges[1:]))

    @pl.when(j == 0)
    def _():
        x = x_ref[...]
        var = jnp.mean(x * x, axis=-1, keepdims=True)
        h = (x * lax.rsqrt(var + EPS) * nw_ref[...]).astype(BF16)
        h_ref[...] = h
        ba_ref[...] = jnp.dot(h, wba_ref[...], preferred_element_type=F32)

    def project(r0, r1, w_ref):
        return jnp.dot(h_ref[r0:r1, :], w_ref[...], preferred_element_type=F32)

    def plain(w_ref):
        for r0 in range(0, tm, slab):
            o_ref[r0:r0 + slab, :] = project(r0, r0 + slab, w_ref).astype(o_ref.dtype)

    def conv_tile(l2_norm):
        seq_start = (i * tm) % rows_per_seq == 0
        prev = jnp.where(seq_start, 0.0, carry_ref[j])
        wv = convw_ref[...]
        for r0, r1 in slabs:
            acc = project(r0, r1, wa_ref)
            window = jnp.concatenate([prev, acc], axis=0)
            shifted = pltpu.roll(window, 1, axis=0)
            far = window * wv[1:2, :] + shifted * wv[0:1, :]
            y = (acc * wv[3:4, :] + shifted[CARRY_ROWS:] * wv[2:3, :]
                 + pltpu.roll(far, 2, axis=0)[CARRY_ROWS:])
            prev = acc[r1 - r0 - CARRY_ROWS:]
            y = _silu(y)
            for h in range(tn // hd):
                yh = y[:, h * hd:(h + 1) * hd]
                if l2_norm:
                    yh = yh * lax.rsqrt(jnp.sum(yh * yh, axis=-1, keepdims=True) + EPS)
                o_ref[r0:r1, h * hd:(h + 1) * hd] = yh.astype(o_ref.dtype)
        carry_ref[j] = prev

    pl.when(j < _TILE_GDN_V)(functools.partial(conv_tile, True))
    pl.when(j == _TILE_GDN_V)(functools.partial(conv_tile, False))

    @pl.when(j == _TILE_GDN_Z)
    def _():
        plain(wa_ref)

    @pl.when((j == _TILE_MOBA_Q) | (j == _TILE_MOBA_K))
    def _():
        w = jnp.where(j == _TILE_MOBA_Q, qkw_ref[0:1, :], qkw_ref[1:2, :])
        for r0, r1 in slabs:
            acc = project(r0, r1, wb_ref)
            cos = cos_ref[r0:r1, :]
            sin = sin_ref[r0:r1, :]
            for h in range(tn // hd):
                x = acc[:, h * hd:(h + 1) * hd]
                y = x * lax.rsqrt(jnp.mean(x * x, axis=-1, keepdims=True) + EPS) * w
                y = y * cos + pltpu.roll(y, hd // 2, axis=1) * sin
                o_ref[r0:r1, h * hd:(h + 1) * hd] = y.astype(o_ref.dtype)

    @pl.when(j > _TILE_MOBA_K)
    def _():
        plain(wb_ref)


def _in_proj(x2, norm_w, w_a, w_b, w_ba, conv_w, cos_t, sin_t, qk_norm_w, tm, tn, t_len):
    m, d = x2.shape
    a_tiles = _TILE_GDN_Z + 1
    n = a_tiles * tn + w_b.shape[1]
    assert t_len % tm == 0 and conv_w.shape == (GDN_CONV, (_TILE_GDN_V + 1) * tn) and GDN_CONV == 4
    assert w_a.shape[1] >= a_tiles * tn and w_b.shape[1] % tn == 0
    b_tiles = w_b.shape[1] // tn
    seq_tiles = t_len // tm
    return pl.pallas_call(
        functools.partial(_in_proj_kernel, rows_per_seq=t_len),
        grid=(m // tm, n // tn),
        in_specs=[
            pl.BlockSpec((tm, d), lambda i, j: (i, 0)),
            pl.BlockSpec((1, d), lambda i, j: (0, 0)),
            pl.BlockSpec((d, tn), lambda i, j: (0, jnp.minimum(j, a_tiles - 1))),
            pl.BlockSpec((d, tn), lambda i, j: (0, jnp.where(j < a_tiles, b_tiles - 1, j - a_tiles))),
            pl.BlockSpec((d, HEAD_DIM), lambda i, j: (0, 0)),
            pl.BlockSpec((GDN_CONV, tn), lambda i, j: (0, jnp.minimum(j, _TILE_GDN_V))),
            pl.BlockSpec((tm, HEAD_DIM), lambda i, j: (i % seq_tiles, 0)),
            pl.BlockSpec((tm, HEAD_DIM), lambda i, j: (i % seq_tiles, 0)),
            pl.BlockSpec((2, HEAD_DIM), lambda i, j: (0, 0)),
        ],
        out_specs=(
            pl.BlockSpec((tm, tn), lambda i, j: (i, j)),
            pl.BlockSpec((tm, HEAD_DIM), lambda i, j: (i, 0)),
        ),
        out_shape=(jax.ShapeDtypeStruct((m, n), BF16),
                   jax.ShapeDtypeStruct((m, HEAD_DIM), F32)),
        scratch_shapes=[
            pltpu.VMEM((tm, d), BF16),
            pltpu.VMEM((_TILE_GDN_V + 1, CARRY_ROWS, tn), F32),
        ],
        compiler_params=pltpu.CompilerParams(
            dimension_semantics=("arbitrary", "arbitrary"), vmem_limit_bytes=VMEM_LIMIT),
        name="in_proj",
    )(x2, norm_w, w_a, w_b, w_ba, conv_w, cos_t, sin_t, qk_norm_w)


def _gdn_kernel(qkv_ref, z_ref, ba_ref, alog_ref, dtb_ref, nw_ref, o_ref, state_ref, *, n_heads):
    tt = qkv_ref.shape[0]
    c = GDN_CHUNK
    n_chunks = tt // c
    hd = HEAD_DIM
    width = n_heads * hd

    @pl.when(pl.program_id(1) == 0)
    def _():
        state_ref[...] = jnp.zeros_like(state_ref)

    ba = ba_ref[...]
    beta_all = _sigmoid(ba)
    sp_in = ba + dtb_ref[...]
    softplus = jnp.maximum(sp_in, 0.0) + jnp.log1p(jnp.exp(-jnp.abs(sp_in)))
    g_all = -jnp.exp(alog_ref[...]) * softplus

    row = lax.broadcasted_iota(jnp.int32, (tt, tt), 0)
    col = lax.broadcasted_iota(jnp.int32, (tt, tt), 1)
    same_chunk = (row // c) == (col // c)
    block_diag = same_chunk.astype(BF16)

    g_hi = g_all.astype(BF16)
    g_rest = g_all - g_hi.astype(F32)
    g_mid = g_rest.astype(BF16)
    g_lo = (g_rest - g_mid.astype(F32)).astype(BF16)
    sums = jnp.dot(jnp.concatenate([(same_chunk & (col <= row)).astype(BF16), block_diag], axis=0),
                   jnp.concatenate([g_hi, g_mid, g_lo], axis=1), preferred_element_type=F32)
    sums = sums[:, :hd] + sums[:, hd:2 * hd] + sums[:, 2 * hd:]
    cg_all = sums[:tt]
    cgl_all = sums[tt:]
    cgt_all = cg_all.T

    prow = lax.broadcasted_iota(jnp.int32, (c, tt), 0)
    plane = lax.broadcasted_iota(jnp.int32, (c, tt), 1)
    pcol = plane % c
    in_chunk = [plane // c == ci for ci in range(n_chunks)]
    causal_p = pcol <= prow
    strict_p = pcol < prow
    same16 = (prow // 16) == (pcol // 16)
    same32 = (prow // 32) == (pcol // 32)
    diag16 = strict_p & same16
    off16 = strict_p & same32 & jnp.logical_not(same16)
    off32 = strict_p & jnp.logical_not(same32)
    eye_p = (pcol == prow).astype(F32)

    def pack(full):
        out = full[(n_chunks - 1) * c:]
        for ci in range(n_chunks - 2, -1, -1):
            out = jnp.where(in_chunk[ci], full[ci * c:(ci + 1) * c], out)
        return out

    def block_diag_of(packed):
        return jnp.concatenate([packed.astype(BF16)] * n_chunks, axis=0) * block_diag

    def mm_packed(packed, bd):
        return jnp.dot(packed.astype(BF16), bd, preferred_element_type=F32)

    scale = hd ** -0.5
    nw = nw_ref[...]
    heads = range(n_heads)

    def each(fn, *per_head):
        return [fn(*args) for args in zip(*per_head)]

    kb = [qkv_ref[:, width + h * hd:width + (h + 1) * hd] for h in heads]
    q = [qkv_ref[:, h * hd:(h + 1) * hd].astype(F32) * scale for h in heads]
    k = each(lambda x: x.astype(F32), kb)
    v = [qkv_ref[:, 2 * width + h * hd:2 * width + (h + 1) * hd].astype(F32) for h in heads]
    beta = [beta_all[:, h:h + 1] for h in heads]
    cg = [cg_all[:, n_heads + h:n_heads + h + 1] for h in heads]
    cgl = [cgl_all[:, n_heads + h:n_heads + h + 1] for h in heads]
    cg_row = [cgt_all[n_heads + h:n_heads + h + 1, :] for h in heads]

    decay = each(lambda a, b: jnp.exp(jnp.where(causal_p, pack(a) - b, -jnp.inf)), cg, cg_row)
    kk = each(lambda x: pack(_mm_nt(x, x)), kb)
    qk = each(lambda x, y, d: block_diag_of(pack(_mm_nt(x, y)) * d), q, kb, decay)
    a_mat = each(lambda b, x, d: jnp.where(strict_p, pack(b) * x * d, 0.0), beta, kk, decay)

    n1 = each(lambda a: -jnp.where(diag16, a, 0.0), a_mat)
    inv = each(lambda n: eye_p + n, n1)
    power = n1
    power_bd = each(block_diag_of, power)
    for _ in range(3):
        power = each(mm_packed, power, power_bd)
        power_bd = each(block_diag_of, power)
        inv = each(lambda t, p: t + mm_packed(t, p), inv, power_bd)
    for off in (off16, off32):
        left = each(lambda t, a: mm_packed(t, block_diag_of(jnp.where(off, a, 0.0))), inv, a_mat)
        inv = each(lambda t, x: t - mm_packed(x, block_diag_of(t)), inv, left)

    ecg = each(jnp.exp, cg)
    rhs = each(lambda vv, kx, b, e: jnp.concatenate([vv * b, kx * (b * e)], axis=1),
               v, k, beta, ecg)
    uw = each(lambda t, x: jnp.dot(block_diag_of(t), x.astype(BF16), preferred_element_type=F32),
              inv, rhs)
    q_dec = each(lambda x, e: x * e, q, ecg)
    k_dec = each(lambda kx, a, b: (kx * jnp.exp(a - b)).astype(BF16), k, cgl, cg)
    g_last = each(jnp.exp, cgl)

    state = [state_ref[h] for h in heads]
    v_new = [[] for _ in heads]
    o_state = [[] for _ in heads]
    for ci in range(n_chunks):
        rs = slice(ci * c, (ci + 1) * c)
        r = each(lambda x, qd, s: _mm(jnp.concatenate([x[rs, hd:], qd[rs]], axis=0), s),
                 uw, q_dec, state)
        vn = each(lambda x, y: x[rs, :hd] - y[:c], uw, r)
        state = each(lambda s, g, kd, x: s * g[ci * c:ci * c + 1, :] + _mm_tn(kd[rs], x),
                     state, g_last, k_dec, vn)
        for h in heads:
            v_new[h].append(vn[h])
            o_state[h].append(r[h][c:])
    for h in heads:
        state_ref[h] = state[h]
    o = each(lambda os, x, vs: jnp.concatenate(os, axis=0) + _mm(x, jnp.concatenate(vs, axis=0)),
             o_state, qk, v_new)

    for h in heads:
        on = o[h] * lax.rsqrt(jnp.mean(o[h] * o[h], axis=-1, keepdims=True) + EPS) * nw
        zz = z_ref[:, h * hd:(h + 1) * hd].astype(F32)
        o_ref[:, h * hd:(h + 1) * hd] = (on * _silu(zz)).astype(o_ref.dtype)


def _gdn(proj, ba, alog_row, dtb_row, norm_w, b_sz, t_len, n_heads, z_block):
    width = n_heads * HEAD_DIM
    tt = min(GDN_TILE, t_len)
    n_t = t_len // tt
    assert t_len % tt == 0 and tt % GDN_CHUNK == 0
    return pl.pallas_call(
        functools.partial(_gdn_kernel, n_heads=n_heads),
        grid=(b_sz, n_t),
        in_specs=[
            pl.BlockSpec((tt, 3 * width), lambda b, t: (b * n_t + t, 0)),
            pl.BlockSpec((tt, width), lambda b, t: (b * n_t + t, z_block)),
            pl.BlockSpec((tt, HEAD_DIM), lambda b, t: (b * n_t + t, 0)),
            pl.BlockSpec((1, HEAD_DIM), lambda b, t: (0, 0)),
            pl.BlockSpec((1, HEAD_DIM), lambda b, t: (0, 0)),
            pl.BlockSpec((1, HEAD_DIM), lambda b, t: (0, 0)),
        ],
        out_specs=pl.BlockSpec((tt, width), lambda b, t: (b * n_t + t, 0)),
        out_shape=jax.ShapeDtypeStruct((b_sz * t_len, width), BF16),
        scratch_shapes=[pltpu.VMEM((n_heads, HEAD_DIM, HEAD_DIM), F32)],
        compiler_params=pltpu.CompilerParams(
            dimension_semantics=("parallel", "arbitrary"), vmem_limit_bytes=VMEM_LIMIT),
        name="gdn",
    )(proj, proj, ba, alog_row, dtb_row, norm_w)


def _moba_kernel(q_ref, k_ref, v_ref, z_ref, o_ref, qaug_s, kaug_s, vt_s, kmean_s, *, unroll):
    t_len, hd = q_ref.shape
    bs = MOBA_BLOCK
    n_blk = t_len // bs
    gate_rows = -(-n_blk // CARRY_ROWS) * CARRY_ROWS
    exp_scale = hd ** -0.5 * math.log2(math.e)

    def rows(i):
        return slice(i * bs, (i + 1) * bs)

    def softmax_weights(s, m):
        return jnp.exp2(((s - m) * exp_scale).astype(BF16))


    def prep_load(i):
        return (i, q_ref[rows(i), :], k_ref[rows(i), :], v_ref[rows(i), :])

    def prep_compute(i, q, k, v):
        lane = lax.broadcasted_iota(jnp.int32, (bs, hd), 1)
        k_aug = jnp.concatenate([k, (lane == i).astype(BF16)], axis=1)
        v_t = jnp.concatenate([v.astype(F32).T, jnp.ones((SUM_ROWS, bs), F32)], axis=0)
        return q, k_aug, jnp.mean(k.astype(F32), axis=0, keepdims=True), v_t.astype(BF16)

    def prep_store(i, q, k_aug, k_mean, v_t):
        qaug_s[rows(i), :hd] = q
        kaug_s[rows(i), :] = k_aug
        kmean_s[i:i + 1, :] = k_mean
        vt_s[:, rows(i)] = v_t

    def gate_load(i):
        return (i, qaug_s[rows(i), :hd], kmean_s[...])

    def gate_compute(i, qi, k_mean):
        blk = lax.broadcasted_iota(jnp.int32, (gate_rows, bs), 0)
        gate = jnp.where(blk < i, _mm_nt(k_mean, qi)[:gate_rows], -jnp.inf)
        sel = blk == i
        for _ in range(MOBA_TOPK):
            best = jnp.max(gate, axis=0, keepdims=True)
            is_best = (gate == best) & (gate > -jnp.inf)
            pick = blk == jnp.min(jnp.where(is_best, blk, hd), axis=0, keepdims=True)
            sel = sel | pick
            gate = jnp.where(pick, -jnp.inf, gate)
        bias_t = jnp.concatenate([jnp.where(sel, 0.0, MASK_BIAS),
                                  jnp.full((hd - gate_rows, bs), MASK_BIAS, F32)], axis=0)
        return (bias_t.T.astype(BF16),)

    def gate_store(i, bias):
        qaug_s[rows(i), hd:] = bias

    def attend_load(i):
        keys = slice(0, (i + 1) * bs)
        return (i, kaug_s[keys, :], qaug_s[rows(i), :], vt_s[:, keys], z_ref[rows(i), :])

    def attend(blocks):
        loaded = [attend_load(i) for i in blocks]
        scores = [_mm_nt(k_aug, q_aug) for _, k_aug, q_aug, _, _ in loaded]
        key = lax.broadcasted_iota(jnp.int32, (bs, bs), 0)
        qry = lax.broadcasted_iota(jnp.int32, (bs, bs), 1)
        weights = []
        for i, s in zip(blocks, scores):
            own = jnp.where(key <= qry, s[i * bs:], -jnp.inf)
            s = jnp.concatenate([s[:i * bs], own], axis=0) if i else own
            weights.append(softmax_weights(s, jnp.max(s, axis=0, keepdims=True)))
        applied = [jnp.dot(v_t, p, preferred_element_type=F32)
                   for (_, _, _, v_t, _), p in zip(loaded, weights)]
        for (i, _, _, _, z), pv in zip(loaded, applied):
            out = (pv[:hd] / pv[hd:hd + 1]).T * _silu(z.astype(F32))
            o_ref[rows(i), :] = out.astype(o_ref.dtype)

    def run(load, compute, store, blocks, group):
        for t in range(0, len(blocks), group):
            loaded = [load(i) for i in blocks[t:t + group]]
            results = [compute(*vals) for vals in loaded]
            for vals, res in zip(loaded, results):
                store(vals[0], *res)

    in_order = list(range(n_blk))
    balanced = [i for pair in zip(reversed(in_order), in_order) for i in pair][:n_blk]
    kmean_s[...] = jnp.zeros_like(kmean_s)
    run(prep_load, prep_compute, prep_store, in_order, unroll)
    run(gate_load, gate_compute, gate_store, in_order, unroll)
    for t in range(0, n_blk, MOBA_ATTEND_GROUP):
        attend(balanced[t:t + MOBA_ATTEND_GROUP])


def _moba(proj, b_sz, t_len, n_heads, q_block0, z_block0):
    width = n_heads * HEAD_DIM
    n_blk = t_len // MOBA_BLOCK
    assert t_len % MOBA_BLOCK == 0 and n_blk <= HEAD_DIM

    def col(base):
        return lambda b, h: (b, base + h)

    return pl.pallas_call(
        functools.partial(_moba_kernel, unroll=MOBA_PREP_GROUP),
        grid=(b_sz, n_heads),
        in_specs=[
            pl.BlockSpec((t_len, HEAD_DIM), col(q_block0)),
            pl.BlockSpec((t_len, HEAD_DIM), col(q_block0 + n_heads)),
            pl.BlockSpec((t_len, HEAD_DIM), col(q_block0 + 2 * n_heads)),
            pl.BlockSpec((t_len, HEAD_DIM), col(z_block0)),
        ],
        out_specs=pl.BlockSpec((t_len, HEAD_DIM), col(0)),
        scratch_shapes=[
            pltpu.VMEM((t_len, 2 * HEAD_DIM), BF16),
            pltpu.VMEM((t_len, 2 * HEAD_DIM), BF16),
            pltpu.VMEM((HEAD_DIM + SUM_ROWS, t_len), BF16),
            pltpu.VMEM((HEAD_DIM, HEAD_DIM), F32),
        ],
        out_shape=jax.ShapeDtypeStruct((b_sz * t_len, width), BF16),
        compiler_params=pltpu.CompilerParams(
            dimension_semantics=("parallel", "parallel"), vmem_limit_bytes=VMEM_LIMIT),
        name="moba",
    )(proj, proj, proj, proj)


def _merge_kernel(x_ref, oa_ref, ob_ref, ga_ref, gb_ref, wa_ref, wb_ref, wo_ref, o_ref):
    ya = jnp.dot(oa_ref[...], wa_ref[...], preferred_element_type=F32)
    yb = jnp.dot(ob_ref[...], wb_ref[...], preferred_element_type=F32)
    merged = (_sigmoid(ga_ref[...].astype(F32)) * ya + _sigmoid(gb_ref[...].astype(F32)) * yb)
    o_ref[...] = x_ref[...] + jnp.dot(merged.astype(BF16), wo_ref[...],
                                      preferred_element_type=F32)


def _merge(x2, oa, ob, proj, wa, wb, wo, tm, gate_block0):
    m, d = x2.shape
    wa_rows, wb_rows = wa.shape[0], wb.shape[0]
    resident = dict(pipeline_mode=pl.Buffered(1))
    return pl.pallas_call(
        _merge_kernel,
        grid=(m // tm,),
        in_specs=[
            pl.BlockSpec((tm, d), lambda i: (i, 0)),
            pl.BlockSpec((tm, wa_rows), lambda i: (i, 0)),
            pl.BlockSpec((tm, wb_rows), lambda i: (i, 0)),
            pl.BlockSpec((tm, d), lambda i: (i, gate_block0)),
            pl.BlockSpec((tm, d), lambda i: (i, gate_block0 + 1)),
            pl.BlockSpec((wa_rows, d), lambda i: (0, 0), **resident),
            pl.BlockSpec((wb_rows, d), lambda i: (0, 0), **resident),
            pl.BlockSpec((d, d), lambda i: (0, 0), **resident),
        ],
        out_specs=pl.BlockSpec((tm, d), lambda i: (i, 0)),
        out_shape=jax.ShapeDtypeStruct((m, d), F32),
        compiler_params=pltpu.CompilerParams(
            dimension_semantics=("parallel",), vmem_limit_bytes=VMEM_LIMIT),
        name="merge_out",
    )(x2, oa, ob, proj, proj, wa, wb, wo)


def _lane_row(vec, offset):
    n = vec.shape[0]
    return jnp.pad(vec.astype(F32), (offset, HEAD_DIM - offset - n)).reshape(1, HEAD_DIM)


def _layer(x2, b_sz, t_len, cos_t, sin_t, norm_w, w_in, conv_w, a_log, dt_bias, gdn_norm_w,
           q_norm_w, k_norm_w, w_out_gdn, w_out_moba, w_o):
    m, d = x2.shape
    gw = w_out_gdn.shape[0]
    mw = w_out_moba.shape[0]
    gh = a_log.shape[0]
    mh = mw // HEAD_DIM
    assert gw == gh * HEAD_DIM and 2 * gh <= HEAD_DIM
    assert gw == mw and (8 * gw) % d == 0 and (2 * d) % gw == 0

    c1 = 4 * gw
    c2 = c1 + 2 * gh
    w_a = w_in.astype(BF16)
    w_b = w_a[:, c2:]
    w_ba = jnp.pad(w_in[:, c1:c2], ((0, 0), (0, HEAD_DIM - 2 * gh))).astype(BF16)
    gdn_z_block = 3
    moba_q_block = _TILE_MOBA_Q * gw // HEAD_DIM
    moba_z_block = 7 * gw // HEAD_DIM
    gate_block = 8 * gw // d

    tm = math.gcd(1024, t_len)
    qk_norm_w = jnp.stack([q_norm_w, k_norm_w]).astype(F32)
    proj, ba = _in_proj(x2, norm_w.reshape(1, d), w_a, w_b, w_ba, conv_w, cos_t, sin_t, qk_norm_w,
                        tm, gw, t_len)

    oa = _gdn(proj, ba, _lane_row(a_log, gh), _lane_row(dt_bias, gh),
              gdn_norm_w.reshape(1, HEAD_DIM), b_sz, t_len, gh, gdn_z_block)
    ob = _moba(proj, b_sz, t_len, mh, moba_q_block, moba_z_block)
    return _merge(x2, oa, ob, proj, w_out_gdn.astype(BF16), w_out_moba.astype(BF16),
                  w_o.astype(BF16), min(MERGE_ROWS, m), gate_block)


def kernel(x, norm_w, w_in, gdn_conv_w, gdn_a_log, gdn_dt_bias, gdn_norm_w, moba_q_norm_w,
           moba_k_norm_w, w_out_gdn, w_out_moba, w_o):
    b_sz, t_len, d = x.shape
    cos_t, sin_t = _rope_tables(t_len)
    x2 = x.reshape(b_sz * t_len, d)
    for l in range(norm_w.shape[0]):
        x2 = _layer(x2, b_sz, t_len, cos_t, sin_t, norm_w[l], w_in[l], gdn_conv_w[l],
                    gdn_a_log[l], gdn_dt_bias[l], gdn_norm_w[l], moba_q_norm_w[l],
                    moba_k_norm_w[l], w_out_gdn[l], w_out_moba[l], w_o[l])
    return x2.reshape(b_sz, t_len, d)
```

```python
import functools
import math

import jax
import jax.numpy as jnp
from jax import lax
from jax.experimental import pallas as pl
from jax.experimental.pallas import tpu as pltpu

F32 = jnp.float32
BF16 = jnp.bfloat16

EPS = 1e-6
ROPE_THETA = 10000.0
HEAD_DIM = 128
GDN_CONV = 4
GDN_CHUNK = 64
GDN_TILE = 256
GDN_TILES_PER_STEP = 2
GDN_HEAD_GROUP = 8
MOBA_BLOCK = 256
MOBA_TOPK = 3
MOBA_PREP_GROUP = 8
MOBA_ATTEND_GROUP = 8
MASK_BIAS = -1e30
MERGE_ROWS = 512
CARRY_ROWS = 8
SUM_ROWS = 16
VMEM_LIMIT = 56 * 1024 * 1024


def _mm(a, b):
    return jnp.dot(a.astype(BF16), b.astype(BF16), preferred_element_type=F32)


def _mm_nt(a, b):
    return lax.dot_general(a.astype(BF16), b.astype(BF16), (((1,), (1,)), ((), ())),
                           preferred_element_type=F32)


def _mm_tn(a, b):
    return lax.dot_general(a.astype(BF16), b.astype(BF16), (((0,), (0,)), ((), ())),
                           preferred_element_type=F32)


def _sigmoid(x):
    return 0.5 * jnp.tanh(0.5 * x) + 0.5


def _silu(x):
    h = 0.5 * x
    return h + h * jnp.tanh(h)


def _rope_table_kernel(cos_ref, sin_ref):
    t_len, d = cos_ref.shape
    lane = lax.broadcasted_iota(jnp.int32, (t_len, d), 1)
    pos = lax.broadcasted_iota(jnp.int32, (t_len, d), 0).astype(F32)
    half = d // 2
    pair = jnp.where(lane < half, lane, lane - half).astype(F32)
    inv_freq = jnp.exp(pair * (-2.0 * math.log(ROPE_THETA) / d))
    ang = pos * inv_freq
    cos_ref[...] = jnp.cos(ang)
    sin_ref[...] = jnp.where(lane < half, -jnp.sin(ang), jnp.sin(ang))


def _rope_tables(t_len):
    return pl.pallas_call(
        _rope_table_kernel,
        out_shape=(jax.ShapeDtypeStruct((t_len, HEAD_DIM), F32),
                   jax.ShapeDtypeStruct((t_len, HEAD_DIM), F32)),
        name="rope_tables",
    )()


_TILE_GDN_V = 2
_TILE_GDN_Z = 3
_TILE_MOBA_Q = 4
_TILE_MOBA_K = 5
EPILOGUE_ROWS = 256


def _in_proj_kernel(x_ref, nw_ref, wa_ref, wb_ref, wba_ref, convw_ref, cos_ref, sin_ref, qkw_ref,
                    o_ref, ba_ref, h_ref, carry_ref, *, rows_per_seq):
    i = pl.program_id(0)
    j = pl.program_id(1)
    tm, tn = o_ref.shape
    hd = HEAD_DIM
    slab = min(EPILOGUE_ROWS, tm)

    @pl.when((i == 0) & (j == 0))
    def _():
        carry_ref[...] = jnp.zeros_like(carry_ref)

    if tm >= 4 * slab:
        edges = [0, slab // 2, *range(slab // 2 + slab, tm - slab // 2, slab), tm - slab // 2, tm]
    else:
        edges = list(range(0, tm + 1, slab))
    slabs = list(zip(edges[:-1], edges[1:]))

    @pl.when(j == 0)
    def _():
        x = x_ref[...]
        var = jnp.mean(x * x, axis=-1, keepdims=True)
        h = (x * lax.rsqrt(var + EPS) * nw_ref[...]).astype(BF16)
        h_ref[...] = h
        ba_ref[...] = jnp.dot(h, wba_ref[...], preferred_element_type=F32)

    def project(r0, r1, w_ref):
        return jnp.dot(h_ref[r0:r1, :], w_ref[...], preferred_element_type=F32)

    def plain(w_ref):
        for r0 in range(0, tm, slab):
            o_ref[r0:r0 + slab, :] = project(r0, r0 + slab, w_ref).astype(o_ref.dtype)

    def conv_tile(l2_norm):
        seq_start = (i * tm) % rows_per_seq == 0
        prev = jnp.where(seq_start, 0.0, carry_ref[j])
        wv = convw_ref[...]
        for r0, r1 in slabs:
            acc = project(r0, r1, wa_ref)
            window = jnp.concatenate([prev, acc], axis=0)
            shifted = pltpu.roll(window, 1, axis=0)
            far = window * wv[1:2, :] + shifted * wv[0:1, :]
            y = (acc * wv[3:4, :] + shifted[CARRY_ROWS:] * wv[2:3, :]
                 + pltpu.roll(far, 2, axis=0)[CARRY_ROWS:])
            prev = acc[r1 - r0 - CARRY_ROWS:]
            y = _silu(y)
            for h in range(tn // hd):
                yh = y[:, h * hd:(h + 1) * hd]
                if l2_norm:
                    yh = yh * lax.rsqrt(jnp.sum(yh * yh, axis=-1, keepdims=True) + EPS)
                o_ref[r0:r1, h * hd:(h + 1) * hd] = yh.astype(o_ref.dtype)
        carry_ref[j] = prev

    pl.when(j < _TILE_GDN_V)(functools.partial(conv_tile, True))
    pl.when(j == _TILE_GDN_V)(functools.partial(conv_tile, False))

    @pl.when(j == _TILE_GDN_Z)
    def _():
        plain(wa_ref)

    @pl.when((j == _TILE_MOBA_Q) | (j == _TILE_MOBA_K))
    def _():
        w = jnp.where(j == _TILE_MOBA_Q, qkw_ref[0:1, :], qkw_ref[1:2, :])
        for r0, r1 in slabs:
            acc = project(r0, r1, wb_ref)
            cos = cos_ref[r0:r1, :]
            sin = sin_ref[r0:r1, :]
            for h in range(tn // hd):
                x = acc[:, h * hd:(h + 1) * hd]
                y = x * lax.rsqrt(jnp.mean(x * x, axis=-1, keepdims=True) + EPS) * w
                y = y * cos + pltpu.roll(y, hd // 2, axis=1) * sin
                o_ref[r0:r1, h * hd:(h + 1) * hd] = y.astype(o_ref.dtype)

    @pl.when(j > _TILE_MOBA_K)
    def _():
        plain(wb_ref)


def _in_proj(x2, norm_w, w_a, w_b, w_ba, conv_w, cos_t, sin_t, qk_norm_w, tm, tn, t_len):
    m, d = x2.shape
    a_tiles = _TILE_GDN_Z + 1
    n = a_tiles * tn + w_b.shape[1]
    assert t_len % tm == 0 and conv_w.shape == (GDN_CONV, (_TILE_GDN_V + 1) * tn) and GDN_CONV == 4
    assert w_a.shape[1] >= a_tiles * tn and w_b.shape[1] % tn == 0
    b_tiles = w_b.shape[1] // tn
    seq_tiles = t_len // tm
    return pl.pallas_call(
        functools.partial(_in_proj_kernel, rows_per_seq=t_len),
        grid=(m // tm, n // tn),
        in_specs=[
            pl.BlockSpec((tm, d), lambda i, j: (i, 0)),
            pl.BlockSpec((1, d), lambda i, j: (0, 0)),
            pl.BlockSpec((d, tn), lambda i, j: (0, jnp.minimum(j, a_tiles - 1))),
            pl.BlockSpec((d, tn), lambda i, j: (0, jnp.where(j < a_tiles, b_tiles - 1, j - a_tiles))),
            pl.BlockSpec((d, HEAD_DIM), lambda i, j: (0, 0)),
            pl.BlockSpec((GDN_CONV, tn), lambda i, j: (0, jnp.minimum(j, _TILE_GDN_V))),
            pl.BlockSpec((tm, HEAD_DIM), lambda i, j: (i % seq_tiles, 0)),
            pl.BlockSpec((tm, HEAD_DIM), lambda i, j: (i % seq_tiles, 0)),
            pl.BlockSpec((2, HEAD_DIM), lambda i, j: (0, 0)),
        ],
        out_specs=(
            pl.BlockSpec((tm, tn), lambda i, j: (i, j)),
            pl.BlockSpec((tm, HEAD_DIM), lambda i, j: (i, 0)),
        ),
        out_shape=(jax.ShapeDtypeStruct((m, n), BF16),
                   jax.ShapeDtypeStruct((m, HEAD_DIM), F32)),
        scratch_shapes=[
            pltpu.VMEM((tm, d), BF16),
            pltpu.VMEM((_TILE_GDN_V + 1, CARRY_ROWS, tn), F32),
        ],
        compiler_params=pltpu.CompilerParams(
            dimension_semantics=("arbitrary", "arbitrary"), vmem_limit_bytes=VMEM_LIMIT),
        name="in_proj",
    )(x2, norm_w, w_a, w_b, w_ba, conv_w, cos_t, sin_t, qk_norm_w)


def _gdn_kernel(qkv_ref, z_ref, ba_ref, alog_ref, dtb_ref, nw_ref, o_ref, state_ref, *, n_heads):
    tt = min(GDN_TILE, qkv_ref.shape[0])
    n_tiles = qkv_ref.shape[0] // tt
    c = GDN_CHUNK
    n_chunks = tt // c
    hd = HEAD_DIM
    width = n_heads * hd

    def tile_rows(t):
        return slice(t * tt, (t + 1) * tt)

    @pl.when(pl.program_id(1) == 0)
    def _():
        state_ref[...] = jnp.zeros_like(state_ref)

    row = lax.broadcasted_iota(jnp.int32, (tt, tt), 0)
    col = lax.broadcasted_iota(jnp.int32, (tt, tt), 1)
    same_chunk = (row // c) == (col // c)
    block_diag = same_chunk.astype(BF16)
    cumulate = jnp.concatenate([(same_chunk & (col <= row)).astype(BF16), block_diag], axis=0)

    def gates(t):
        ba = ba_ref[tile_rows(t), :]
        sp_in = ba + dtb_ref[...]
        softplus = jnp.maximum(sp_in, 0.0) + jnp.log1p(jnp.exp(-jnp.abs(sp_in)))
        g_all = -jnp.exp(alog_ref[...]) * softplus
        g_hi = g_all.astype(BF16)
        g_rest = g_all - g_hi.astype(F32)
        g_mid = g_rest.astype(BF16)
        g_lo = (g_rest - g_mid.astype(F32)).astype(BF16)
        sums = jnp.dot(cumulate, jnp.concatenate([g_hi, g_mid, g_lo], axis=1),
                       preferred_element_type=F32)
        sums = sums[:, :hd] + sums[:, hd:2 * hd] + sums[:, 2 * hd:]
        return _sigmoid(ba), sums[:tt], sums[tt:], sums[:tt].T

    gate = [gates(t) for t in range(n_tiles)]

    prow = lax.broadcasted_iota(jnp.int32, (c, tt), 0)
    plane = lax.broadcasted_iota(jnp.int32, (c, tt), 1)
    pcol = plane % c
    in_chunk = [plane // c == ci for ci in range(n_chunks)]
    causal_p = pcol <= prow
    strict_p = pcol < prow
    same16 = (prow // 16) == (pcol // 16)
    same32 = (prow // 32) == (pcol // 32)
    diag16 = strict_p & same16
    off16 = strict_p & same32 & jnp.logical_not(same16)
    off32 = strict_p & jnp.logical_not(same32)
    eye_p = (pcol == prow).astype(F32)

    def pack(full):
        out = full[(n_chunks - 1) * c:]
        for ci in range(n_chunks - 2, -1, -1):
            out = jnp.where(in_chunk[ci], full[ci * c:(ci + 1) * c], out)
        return out

    def block_diag_of(packed):
        return jnp.concatenate([packed.astype(BF16)] * n_chunks, axis=0) * block_diag

    def mm_packed(packed, bd):
        return jnp.dot(packed.astype(BF16), bd, preferred_element_type=F32)

    scale = hd ** -0.5
    nw = nw_ref[...]

    def each(fn, *per_head):
        return [fn(*args) for args in zip(*per_head)]

    def delta_rule(heads):
        units = [(t, h) for t in range(n_tiles) for h in heads]

        def head_cols(ref, t, h, section=0):
            return ref[tile_rows(t), section * width + h * hd:section * width + (h + 1) * hd]

        kb = [head_cols(qkv_ref, t, h, 1) for t, h in units]
        q = [head_cols(qkv_ref, t, h, 0).astype(F32) * scale for t, h in units]
        k = each(lambda x: x.astype(F32), kb)
        v = [head_cols(qkv_ref, t, h, 2).astype(F32) for t, h in units]
        beta = [gate[t][0][:, h:h + 1] for t, h in units]
        cg = [gate[t][1][:, n_heads + h:n_heads + h + 1] for t, h in units]
        cgl = [gate[t][2][:, n_heads + h:n_heads + h + 1] for t, h in units]
        cg_row = [gate[t][3][n_heads + h:n_heads + h + 1, :] for t, h in units]

        decay = each(lambda a, b: jnp.exp(jnp.where(causal_p, pack(a) - b, -jnp.inf)), cg, cg_row)
        kk = each(lambda x: pack(_mm_nt(x, x)), kb)
        qk = each(lambda x, y, d: block_diag_of(pack(_mm_nt(x, y)) * d), q, kb, decay)
        a_mat = each(lambda b, x, d: jnp.where(strict_p, pack(b) * x * d, 0.0), beta, kk, decay)

        n1 = each(lambda a: -jnp.where(diag16, a, 0.0), a_mat)
        inv = each(lambda n: eye_p + n, n1)
        power = n1
        power_bd = each(block_diag_of, power)
        for _ in range(3):
            power = each(mm_packed, power, power_bd)
            power_bd = each(block_diag_of, power)
            inv = each(lambda t, p: t + mm_packed(t, p), inv, power_bd)
        for off in (off16, off32):
            left = each(lambda t, a: mm_packed(t, block_diag_of(jnp.where(off, a, 0.0))),
                        inv, a_mat)
            inv = each(lambda t, x: t - mm_packed(x, block_diag_of(t)), inv, left)

        ecg = each(jnp.exp, cg)
        rhs = each(lambda vv, kx, b, e: jnp.concatenate([vv * b, kx * (b * e)], axis=1),
                   v, k, beta, ecg)
        uw = each(lambda t, x: jnp.dot(block_diag_of(t), x.astype(BF16),
                                       preferred_element_type=F32), inv, rhs)
        q_dec = each(lambda x, e: x * e, q, ecg)
        k_dec = each(lambda kx, a, b: (kx * jnp.exp(a - b)).astype(BF16), k, cgl, cg)
        g_last = each(jnp.exp, cgl)

        state = [state_ref[h] for h in heads]
        for t in range(n_tiles):
            mine = slice(t * len(heads), (t + 1) * len(heads))
            v_new = [[] for _ in heads]
            o_state = [[] for _ in heads]
            for ci in range(n_chunks):
                rs = slice(ci * c, (ci + 1) * c)
                r = each(lambda x, qd, s: _mm(jnp.concatenate([x[rs, hd:], qd[rs]], axis=0), s),
                         uw[mine], q_dec[mine], state)
                vn = each(lambda x, y: x[rs, :hd] - y[:c], uw[mine], r)
                state = each(lambda s, g, kd, x: s * g[ci * c:ci * c + 1, :] + _mm_tn(kd[rs], x),
                             state, g_last[mine], k_dec[mine], vn)
                for n in range(len(heads)):
                    v_new[n].append(vn[n])
                    o_state[n].append(r[n][c:])
            o = each(lambda os, x, vs: (jnp.concatenate(os, axis=0)
                                        + _mm(x, jnp.concatenate(vs, axis=0))),
                     o_state, qk[mine], v_new)
            for n, h in enumerate(heads):
                on = o[n] * lax.rsqrt(jnp.mean(o[n] * o[n], axis=-1, keepdims=True) + EPS) * nw
                zz = head_cols(z_ref, t, h).astype(F32)
                o_ref[tile_rows(t), h * hd:(h + 1) * hd] = (on * _silu(zz)).astype(o_ref.dtype)
        for n, h in enumerate(heads):
            state_ref[h] = state[n]

    for h0 in range(0, n_heads, GDN_HEAD_GROUP):
        delta_rule(range(h0, min(h0 + GDN_HEAD_GROUP, n_heads)))


def _gdn(proj, ba, alog_row, dtb_row, norm_w, b_sz, t_len, n_heads, z_block):
    width = n_heads * HEAD_DIM
    tt = min(GDN_TILE * GDN_TILES_PER_STEP, t_len)
    n_t = t_len // tt
    assert t_len % tt == 0 and tt % min(GDN_TILE, tt) == 0 and GDN_TILE % GDN_CHUNK == 0
    return pl.pallas_call(
        functools.partial(_gdn_kernel, n_heads=n_heads),
        grid=(b_sz, n_t),
        in_specs=[
            pl.BlockSpec((tt, 3 * width), lambda b, t: (b * n_t + t, 0)),
            pl.BlockSpec((tt, width), lambda b, t: (b * n_t + t, z_block)),
            pl.BlockSpec((tt, HEAD_DIM), lambda b, t: (b * n_t + t, 0)),
            pl.BlockSpec((1, HEAD_DIM), lambda b, t: (0, 0)),
            pl.BlockSpec((1, HEAD_DIM), lambda b, t: (0, 0)),
            pl.BlockSpec((1, HEAD_DIM), lambda b, t: (0, 0)),
        ],
        out_specs=pl.BlockSpec((tt, width), lambda b, t: (b * n_t + t, 0)),
        out_shape=jax.ShapeDtypeStruct((b_sz * t_len, width), BF16),
        scratch_shapes=[pltpu.VMEM((n_heads, HEAD_DIM, HEAD_DIM), F32)],
        compiler_params=pltpu.CompilerParams(
            dimension_semantics=("parallel", "arbitrary"), vmem_limit_bytes=VMEM_LIMIT),
        name="gdn",
    )(proj, proj, ba, alog_row, dtb_row, norm_w)


def _moba_kernel(q_ref, k_ref, v_ref, z_ref, o_ref, qaug_s, kaug_s, vt_s, kmean_s, *, unroll):
    t_len, hd = q_ref.shape
    bs = MOBA_BLOCK
    n_blk = t_len // bs
    gate_rows = -(-n_blk // CARRY_ROWS) * CARRY_ROWS
    exp_scale = hd ** -0.5 * math.log2(math.e)

    def rows(i):
        return slice(i * bs, (i + 1) * bs)

    def softmax_weights(s, m):
        return jnp.exp2(((s - m) * exp_scale).astype(BF16))


    def prep_load(i):
        return (i, q_ref[rows(i), :], k_ref[rows(i), :], v_ref[rows(i), :])

    def prep_compute(i, q, k, v):
        lane = lax.broadcasted_iota(jnp.int32, (bs, hd), 1)
        k_aug = jnp.concatenate([k, (lane == i).astype(BF16)], axis=1)
        v_t = jnp.concatenate([v.astype(F32).T, jnp.ones((SUM_ROWS, bs), F32)], axis=0)
        return q, k_aug, jnp.mean(k.astype(F32), axis=0, keepdims=True), v_t.astype(BF16)

    def prep_store(i, q, k_aug, k_mean, v_t):
        qaug_s[rows(i), :hd] = q
        kaug_s[rows(i), :] = k_aug
        kmean_s[i:i + 1, :] = k_mean
        vt_s[:, rows(i)] = v_t

    def gate_load(i):
        return (i, qaug_s[rows(i), :hd], kmean_s[...])

    def gate_compute(i, qi, k_mean):
        blk = lax.broadcasted_iota(jnp.int32, (gate_rows, bs), 0)
        gate = jnp.where(blk < i, _mm_nt(k_mean, qi)[:gate_rows], -jnp.inf)
        sel = blk == i
        for _ in range(MOBA_TOPK):
            best = jnp.max(gate, axis=0, keepdims=True)
            is_best = (gate == best) & (gate > -jnp.inf)
            pick = blk == jnp.min(jnp.where(is_best, blk, hd), axis=0, keepdims=True)
            sel = sel | pick
            gate = jnp.where(pick, -jnp.inf, gate)
        bias_t = jnp.concatenate([jnp.where(sel, 0.0, MASK_BIAS),
                                  jnp.full((hd - gate_rows, bs), MASK_BIAS, F32)], axis=0)
        return (bias_t.T.astype(BF16),)

    def gate_store(i, bias):
        qaug_s[rows(i), hd:] = bias

    def attend_load(i):
        keys = slice(0, (i + 1) * bs)
        return (i, kaug_s[keys, :], qaug_s[rows(i), :], vt_s[:, keys], z_ref[rows(i), :])

    def attend(blocks):
        loaded = [attend_load(i) for i in blocks]
        scores = [_mm_nt(k_aug, q_aug) for _, k_aug, q_aug, _, _ in loaded]
        key = lax.broadcasted_iota(jnp.int32, (bs, bs), 0)
        qry = lax.broadcasted_iota(jnp.int32, (bs, bs), 1)
        weights = []
        for i, s in zip(blocks, scores):
            own = jnp.where(key <= qry, s[i * bs:], -jnp.inf)
            s = jnp.concatenate([s[:i * bs], own], axis=0) if i else own
            weights.append(softmax_weights(s, jnp.max(s, axis=0, keepdims=True)))
        applied = [jnp.dot(v_t, p, preferred_element_type=F32)
                   for (_, _, _, v_t, _), p in zip(loaded, weights)]
        for (i, _, _, _, z), pv in zip(loaded, applied):
            out = (pv[:hd] / pv[hd:hd + 1]).T * _silu(z.astype(F32))
            o_ref[rows(i), :] = out.astype(o_ref.dtype)

    def run(load, compute, store, blocks, group):
        for t in range(0, len(blocks), group):
            loaded = [load(i) for i in blocks[t:t + group]]
            results = [compute(*vals) for vals in loaded]
            for vals, res in zip(loaded, results):
                store(vals[0], *res)

    in_order = list(range(n_blk))
    balanced = [i for pair in zip(reversed(in_order), in_order) for i in pair][:n_blk]
    kmean_s[...] = jnp.zeros_like(kmean_s)
    run(prep_load, prep_compute, prep_store, in_order, unroll)
    run(gate_load, gate_compute, gate_store, in_order, unroll)
    for t in range(0, n_blk, MOBA_ATTEND_GROUP):
        attend(balanced[t:t + MOBA_ATTEND_GROUP])


def _moba(proj, b_sz, t_len, n_heads, q_block0, z_block0):
    width = n_heads * HEAD_DIM
    n_blk = t_len // MOBA_BLOCK
    assert t_len % MOBA_BLOCK == 0 and n_blk <= HEAD_DIM

    def col(base):
        return lambda b, h: (b, base + h)

    return pl.pallas_call(
        functools.partial(_moba_kernel, unroll=MOBA_PREP_GROUP),
        grid=(b_sz, n_heads),
        in_specs=[
            pl.BlockSpec((t_len, HEAD_DIM), col(q_block0)),
            pl.BlockSpec((t_len, HEAD_DIM), col(q_block0 + n_heads)),
            pl.BlockSpec((t_len, HEAD_DIM), col(q_block0 + 2 * n_heads)),
            pl.BlockSpec((t_len, HEAD_DIM), col(z_block0)),
        ],
        out_specs=pl.BlockSpec((t_len, HEAD_DIM), col(0)),
        scratch_shapes=[
            pltpu.VMEM((t_len, 2 * HEAD_DIM), BF16),
            pltpu.VMEM((t_len, 2 * HEAD_DIM), BF16),
            pltpu.VMEM((HEAD_DIM + SUM_ROWS, t_len), BF16),
            pltpu.VMEM((HEAD_DIM, HEAD_DIM), F32),
        ],
        out_shape=jax.ShapeDtypeStruct((b_sz * t_len, width), BF16),
        compiler_params=pltpu.CompilerParams(
            dimension_semantics=("parallel", "parallel"), vmem_limit_bytes=VMEM_LIMIT),
        name="moba",
    )(proj, proj, proj, proj)


def _merge_kernel(x_ref, oa_ref, ob_ref, ga_ref, gb_ref, wa_ref, wb_ref, wo_ref, o_ref):
    ya = jnp.dot(oa_ref[...], wa_ref[...], preferred_element_type=F32)
    yb = jnp.dot(ob_ref[...], wb_ref[...], preferred_element_type=F32)
    merged = (_sigmoid(ga_ref[...].astype(F32)) * ya + _sigmoid(gb_ref[...].astype(F32)) * yb)
    o_ref[...] = x_ref[...] + jnp.dot(merged.astype(BF16), wo_ref[...],
                                      preferred_element_type=F32)


def _merge(x2, oa, ob, proj, wa, wb, wo, tm, gate_block0):
    m, d = x2.shape
    wa_rows, wb_rows = wa.shape[0], wb.shape[0]
    resident = dict(pipeline_mode=pl.Buffered(1))
    return pl.pallas_call(
        _merge_kernel,
        grid=(m // tm,),
        in_specs=[
            pl.BlockSpec((tm, d), lambda i: (i, 0)),
            pl.BlockSpec((tm, wa_rows), lambda i: (i, 0)),
            pl.BlockSpec((tm, wb_rows), lambda i: (i, 0)),
            pl.BlockSpec((tm, d), lambda i: (i, gate_block0)),
            pl.BlockSpec((tm, d), lambda i: (i, gate_block0 + 1)),
            pl.BlockSpec((wa_rows, d), lambda i: (0, 0), **resident),
            pl.BlockSpec((wb_rows, d), lambda i: (0, 0), **resident),
            pl.BlockSpec((d, d), lambda i: (0, 0), **resident),
        ],
        out_specs=pl.BlockSpec((tm, d), lambda i: (i, 0)),
        out_shape=jax.ShapeDtypeStruct((m, d), F32),
        compiler_params=pltpu.CompilerParams(
            dimension_semantics=("parallel",), vmem_limit_bytes=VMEM_LIMIT),
        name="merge_out",
    )(x2, oa, ob, proj, proj, wa, wb, wo)


def _lane_row(vec, offset):
    n = vec.shape[0]
    return jnp.pad(vec.astype(F32), (offset, HEAD_DIM - offset - n)).reshape(1, HEAD_DIM)


def _layer(x2, b_sz, t_len, cos_t, sin_t, norm_w, w_in, conv_w, a_log, dt_bias, gdn_norm_w,
           q_norm_w, k_norm_w, w_out_gdn, w_out_moba, w_o):
    m, d = x2.shape
    gw = w_out_gdn.shape[0]
    mw = w_out_moba.shape[0]
    gh = a_log.shape[0]
    mh = mw // HEAD_DIM
    assert gw == gh * HEAD_DIM and 2 * gh <= HEAD_DIM
    assert gw == mw and (8 * gw) % d == 0 and (2 * d) % gw == 0

    c1 = 4 * gw
    c2 = c1 + 2 * gh
    w_a = w_in.astype(BF16)
    w_b = w_a[:, c2:]
    w_ba = jnp.pad(w_in[:, c1:c2], ((0, 0), (0, HEAD_DIM - 2 * gh))).astype(BF16)
    gdn_z_block = 3
    moba_q_block = _TILE_MOBA_Q * gw // HEAD_DIM
    moba_z_block = 7 * gw // HEAD_DIM
    gate_block = 8 * gw // d

    tm = math.gcd(1024, t_len)
    qk_norm_w = jnp.stack([q_norm_w, k_norm_w]).astype(F32)
    proj, ba = _in_proj(x2, norm_w.reshape(1, d), w_a, w_b, w_ba, conv_w, cos_t, sin_t, qk_norm_w,
                        tm, gw, t_len)

    oa = _gdn(proj, ba, _lane_row(a_log, gh), _lane_row(dt_bias, gh),
              gdn_norm_w.reshape(1, HEAD_DIM), b_sz, t_len, gh, gdn_z_block)
    ob = _moba(proj, b_sz, t_len, mh, moba_q_block, moba_z_block)
    return _merge(x2, oa, ob, proj, w_out_gdn.astype(BF16), w_out_moba.astype(BF16),
                  w_o.astype(BF16), min(MERGE_ROWS, m), gate_block)


def kernel(x, norm_w, w_in, gdn_conv_w, gdn_a_log, gdn_dt_bias, gdn_norm_w, moba_q_norm_w,
           moba_k_norm_w, w_out_gdn, w_out_moba, w_o):
    b_sz, t_len, d = x.shape
    cos_t, sin_t = _rope_tables(t_len)
    x2 = x.reshape(b_sz * t_len, d)
    for l in range(norm_w.shape[0]):
        x2 = _layer(x2, b_sz, t_len, cos_t, sin_t, norm_w[l], w_in[l], gdn_conv_w[l],
                    gdn_a_log[l], gdn_dt_bias[l], gdn_norm_w[l], moba_q_norm_w[l],
                    moba_k_norm_w[l], w_out_gdn[l], w_out_moba[l], w_o[l])
    return x2.reshape(b_sz, t_len, d)
```
